```python
import math
import jax, jax.numpy as jnp
from jax import lax
import numpy as np

D_MODEL = 1024
BATCH = 2
SEQ = 8192
DEPTH = 2
DEC_BATCH = 128
DEC_SEQ = 1
PAST_LEN = 8192
PAGE_SIZE = 128

HEAD_DIM = 64
MLA_HEADS = 6
MLA_Q_LORA = 256
MLA_KV_LORA = 128
MLA_NOPE = 64
MLA_ROPE = 32
MLA_V = 64
MLA_QK = MLA_NOPE + MLA_ROPE
ROPE_THETA = 10000.0
NSA_HEADS = 6
NSA_KV_GROUPS = 2
NSA_HPG = NSA_HEADS // NSA_KV_GROUPS
CMP_BLOCK = 32
CMP_STRIDE = 16
CMP_HID = 128
SEL_BLOCK = 64
SEL_TOP_N = 16
WINDOW = 512
FORCE_BONUS = 1000.0
MEM_HEADS = 4
MEM_TOKENS = 256
REL_BUCKETS = 32
REL_MAX_DIST = 128

QBLOCK = 128
RMS_EPS = 1e-6
NEG = -1e30
MIX_A = MLA_HEADS * MLA_V
MIX_B = NSA_HEADS * HEAD_DIM
MIX_M = MEM_HEADS * HEAD_DIM
MIX = MIX_A + MIX_B + MIX_M
IN_SPLITS = (MLA_Q_LORA, MLA_KV_LORA, MLA_ROPE, MIX_A,
             MIX_B, 6 * NSA_KV_GROUPS * HEAD_DIM, 3 * NSA_HEADS, MIX_B,
             MIX_M, MIX_M)
IN_W = sum(IN_SPLITS)

kernel_name = "hymba_mla_nsa_mem_decode_step"


def rms_norm(x, g):
    xf = x.astype(jnp.float32)
    y = xf * lax.rsqrt(jnp.mean(xf * xf, -1, keepdims=True) + RMS_EPS)
    return (y * g.astype(jnp.float32)).astype(x.dtype)


def rope(x, pos):
    half = MLA_ROPE // 2
    inv = ROPE_THETA ** (-jnp.arange(half, dtype=jnp.float32) / half)
    ang = pos.astype(jnp.float32)[:, None] * inv[None, :]
    shape = (1, ang.shape[0]) + (1,) * (x.ndim - 3) + (half,)
    cos = jnp.cos(ang).reshape(shape)
    sin = jnp.sin(ang).reshape(shape)
    x1, x2 = x[..., :half], x[..., half:]
    return jnp.concatenate([x1 * cos - x2 * sin, x2 * cos + x1 * sin], -1).astype(x.dtype)


def rel_bucket(dist):
    n = jnp.maximum(dist, 0)
    exact = REL_BUCKETS // 2
    log_part = jnp.log(jnp.maximum(n, 1).astype(jnp.float32) / exact) / math.log(REL_MAX_DIST / exact)
    large = jnp.minimum(exact + (log_part * (REL_BUCKETS - exact)).astype(jnp.int32), REL_BUCKETS - 1)
    return jnp.where(n < exact, n, large)


def masked_softmax(s, mask):
    s = jnp.where(mask, s.astype(jnp.float32), NEG)
    m = jnp.max(s, -1, keepdims=True)
    e = jnp.where(mask, jnp.exp(s - m), 0.0)
    return e / jnp.maximum(jnp.sum(e, -1, keepdims=True), 1e-30)


def prep(x, pos, lw):
    N, T, _ = x.shape
    h = rms_norm(x, lw['norm_g']) @ lw['w_in']
    pts = np.cumsum(IN_SPLITS)[:-1].tolist()
    q_lat, kv_lat, kr_raw, gate_a, nq, nkv, nsa_g, gate_b, mq, gate_m = jnp.split(h, pts, axis=-1)
    q = (rms_norm(q_lat, lw['mla_q_norm']) @ lw['mla_w_uq']).reshape(N, T, MLA_HEADS, MLA_QK)
    q_a = jnp.concatenate([rms_norm(q[..., :MLA_NOPE], lw['mla_nope_g'][0]),
                           rope(rms_norm(q[..., MLA_NOPE:], lw['mla_rope_g'][0]), pos)], -1)
    rows_a = jnp.concatenate([rms_norm(kv_lat, lw['mla_kv_norm']),
                              rope(rms_norm(kr_raw, lw['mla_rope_g'][1]), pos)], -1)
    q_b = rms_norm(nq.reshape(N, T, NSA_HEADS, HEAD_DIM), lw['nsa_qk_g'][0])
    kv = nkv.reshape(N, T, 3, 2, NSA_KV_GROUPS, HEAD_DIM)
    rows_cmp = kv[:, :, 0]
    rows_slc = jnp.stack([rms_norm(kv[:, :, 1, 0], lw['nsa_qk_g'][2]), kv[:, :, 1, 1]], 2)
    rows_win = jnp.stack([rms_norm(kv[:, :, 2, 0], lw['nsa_qk_g'][3]), kv[:, :, 2, 1]], 2)
    g_b = jax.nn.sigmoid(nsa_g.astype(jnp.float32)).astype(x.dtype).reshape(N, T, NSA_HEADS, 3)
    q_m = rms_norm(mq.reshape(N, T, MEM_HEADS, HEAD_DIM), lw['mem_qk_g'][0])
    return dict(q_a=q_a, rows_a=rows_a, q_b=q_b, rows_cmp=rows_cmp, rows_slc=rows_slc,
                rows_win=rows_win, g_b=g_b, q_m=q_m, gate_a=gate_a, gate_b=gate_b, gate_m=gate_m)


def merge(x, o_a, o_b, o_m, p, lw):
    z = jnp.concatenate([o_a * jax.nn.silu(p['gate_a']), o_b * jax.nn.silu(p['gate_b']),
                         o_m * jax.nn.silu(p['gate_m'])], -1)
    return x + z @ lw['w_out']


def mla_kv_from_rows(rows, lw):
    N, L, _ = rows.shape
    c, kr = rows[..., :MLA_KV_LORA], rows[..., MLA_KV_LORA:]
    kv = jnp.einsum('nlc,chd->nlhd', c, lw['mla_w_ukv'].reshape(MLA_KV_LORA, MLA_HEADS, MLA_NOPE + MLA_V))
    k_nope = rms_norm(kv[..., :MLA_NOPE], lw['mla_nope_g'][1])
    k = jnp.concatenate([k_nope, jnp.broadcast_to(kr[:, :, None, :], (N, L, MLA_HEADS, MLA_ROPE))], -1)
    return k, kv[..., MLA_NOPE:]


def mla_attend(q, q_pos, k, v, k_pos):
    s = jnp.einsum('nqhd,nkhd->nhqk', q, k) * (MLA_QK ** -0.5)
    p = masked_softmax(s, k_pos[None, :] <= q_pos[:, None])
    return jnp.einsum('nhqk,nkhd->nqhd', p.astype(v.dtype), v)


def nsa_compress(rows, lw):
    N, L = rows.shape[:2]
    r = CMP_BLOCK // CMP_STRIDE
    n_sub = L // CMP_STRIDE
    C = n_sub - r + 1
    sub = rows[:, :n_sub * CMP_STRIDE].reshape((N, n_sub, CMP_STRIDE) + rows.shape[2:])
    blk = jnp.concatenate([sub[:, j:j + C] for j in range(r)], axis=2)
    blk = blk + jnp.swapaxes(lw['nsa_cmp_pe'], 0, 1)[:, :, None, :]
    flat = blk.transpose(0, 1, 3, 4, 2, 5).reshape(N, C, 2, NSA_KV_GROUPS, CMP_BLOCK * HEAD_DIM)
    hid = jax.nn.gelu(jnp.einsum('ncsgf,sfe->ncsge', flat, lw['nsa_cmp_w1']) + lw['nsa_cmp_b1'][:, None, :])
    out = jnp.einsum('ncsge,sed->ncsgd', hid, lw['nsa_cmp_w2'])
    kc = rms_norm(out[:, :, 0], lw['nsa_qk_g'][1])
    c_end = jnp.arange(C) * CMP_STRIDE + (CMP_BLOCK - 1)
    return kc, out[:, :, 1], c_end


def cmp_to_sel(C, n_sel):
    start = jnp.arange(C) * CMP_STRIDE
    j0 = jnp.arange(n_sel) * SEL_BLOCK
    return ((start[:, None] < j0[None, :] + SEL_BLOCK) &
            (start[:, None] + CMP_BLOCK > j0[None, :])).astype(jnp.float32)


def nsa_core(q, q_pos, kc, vc, c_end, fetch, n_sel, kw, vw, w_pos, gates, rel_bias):
    N, Q = q.shape[:2]
    G = NSA_KV_GROUPS
    qg = q.reshape(N, Q, G, NSA_HPG, HEAD_DIM) * (HEAD_DIM ** -0.5)
    tbl = rel_bias.reshape(REL_BUCKETS, G, NSA_HPG)
    t = q_pos[:, None]
    sc = jnp.einsum('nqghd,ncgd->nqghc', qg, kc)
    sc = sc + tbl[rel_bucket(t - c_end[None, :])].transpose(0, 2, 3, 1)[None]
    pc = masked_softmax(sc, (c_end[None, :] <= t)[None, :, None, None, :])
    oc = jnp.einsum('nqghc,ncgd->nqghd', pc.astype(vc.dtype), vc)
    imp = jnp.einsum('nqgc,cj->nqgj', jnp.sum(pc, 3), cmp_to_sel(kc.shape[1], n_sel))
    jj = jnp.arange(n_sel)
    cur = (q_pos // SEL_BLOCK)[:, None]
    forced = ((jj[None] == 0) | (jj[None] == cur) | (jj[None] == cur - 1)).astype(jnp.float32)
    valid = (jj[None] * SEL_BLOCK) <= t
    imp = jnp.where(valid[None, :, None, :], imp + FORCE_BONUS * forced[None, :, None, :], NEG)
    _, idx = lax.top_k(imp, min(SEL_TOP_N, n_sel))
    ks, vs = fetch(idx)
    s_pos = idx[..., None] * SEL_BLOCK + jnp.arange(SEL_BLOCK)
    g5 = jnp.arange(G)[None, None, :, None, None]
    bs = tbl[rel_bucket(q_pos[None, :, None, None, None] - s_pos), g5].transpose(0, 1, 2, 5, 3, 4)
    ss = jnp.einsum('nqghd,nqgtsd->nqghts', qg, ks) + bs
    kk = ks.shape[3] * SEL_BLOCK
    ms = (s_pos <= q_pos[None, :, None, None, None])[:, :, :, None].reshape(N, Q, G, 1, kk)
    ps = masked_softmax(ss.reshape(N, Q, G, NSA_HPG, kk), ms)
    osl = jnp.einsum('nqghm,nqgmd->nqghd', ps.astype(vs.dtype), vs.reshape(N, Q, G, kk, HEAD_DIM))
    dw = t - w_pos[None, :]
    sw = jnp.einsum('nqghd,nkgd->nqghk', qg, kw) + tbl[rel_bucket(dw)].transpose(0, 2, 3, 1)[None]
    mw = ((dw >= 0) & (dw < WINDOW) & (w_pos[None, :] >= 0))[None, :, None, None, :]
    ow = jnp.einsum('nqghk,nkgd->nqghd', masked_softmax(sw, mw).astype(vw.dtype), vw)
    g = gates.reshape(N, Q, G, NSA_HPG, 3)
    o = oc * g[..., 0:1] + osl * g[..., 1:2] + ow * g[..., 2:3]
    return o.reshape(N, Q, NSA_HEADS * HEAD_DIM)


def mem_kv(mem, lw):
    N, M, _ = mem.shape
    kv = (rms_norm(mem, lw['mem_norm_g']) @ lw['w_mem_kv']).reshape(N, M, 2, MEM_HEADS, HEAD_DIM)
    return jnp.stack([rms_norm(kv[:, :, 0], lw['mem_qk_g'][1]), kv[:, :, 1]], 2)


def mem_attend(q, kv):
    N, T = q.shape[:2]
    s = jnp.einsum('nthd,nmhd->nhtm', q, kv[:, :, 0]) * (HEAD_DIM ** -0.5)
    p = jax.nn.softmax(s.astype(jnp.float32), -1)
    return jnp.einsum('nhtm,nmhd->nthd', p.astype(kv.dtype), kv[:, :, 1]).reshape(N, T, MIX_M)


def prompt_layer(x, mem, lw, rel_bias):
    N, T, _ = x.shape
    pos = jnp.arange(T)
    p = prep(x, pos, lw)
    nqb = T // QBLOCK
    qidx = jnp.arange(nqb)
    k_a, v_a = mla_kv_from_rows(p['rows_a'], lw)
    qa_b = p['q_a'].reshape(N, nqb, QBLOCK, MLA_HEADS, MLA_QK).swapaxes(0, 1)
    o_a = lax.map(lambda a: mla_attend(a[0], a[1] * QBLOCK + jnp.arange(QBLOCK), k_a, v_a, pos), (qa_b, qidx))
    o_a = o_a.swapaxes(0, 1).reshape(N, T, MIX_A)
    kc, vc, c_end = nsa_compress(p['rows_cmp'], lw)
    n_sel = -(-T // SEL_BLOCK)
    slc = jnp.pad(p['rows_slc'], ((0, 0), (0, n_sel * SEL_BLOCK - T), (0, 0), (0, 0), (0, 0)))
    slc_blocks = slc.reshape(N, n_sel, SEL_BLOCK, 2, NSA_KV_GROUPS, HEAD_DIM).transpose(0, 1, 4, 2, 3, 5)

    def fetch(idx):
        n = jnp.arange(N)[:, None, None, None]
        g = jnp.arange(NSA_KV_GROUPS)[None, None, :, None]
        blk = slc_blocks[n, idx, g]
        return blk[..., 0, :], blk[..., 1, :]

    win_pad = jnp.pad(p['rows_win'], ((0, 0), (WINDOW, 0), (0, 0), (0, 0), (0, 0)))

    def nsa_block(a):
        qb, gb, qi = a
        qpos = qi * QBLOCK + jnp.arange(QBLOCK)
        kw = lax.dynamic_slice_in_dim(win_pad, qi * QBLOCK, WINDOW + QBLOCK, axis=1)
        wpos = qi * QBLOCK - WINDOW + jnp.arange(WINDOW + QBLOCK)
        return nsa_core(qb, qpos, kc, vc, c_end, fetch, n_sel, kw[:, :, 0], kw[:, :, 1], wpos, gb, rel_bias)

    qb_b = p['q_b'].reshape(N, nqb, QBLOCK, NSA_HEADS, HEAD_DIM).swapaxes(0, 1)
    gb_b = p['g_b'].reshape(N, nqb, QBLOCK, NSA_HEADS, 3).swapaxes(0, 1)
    o_b = lax.map(nsa_block, (qb_b, gb_b, qidx)).swapaxes(0, 1).reshape(N, T, MIX_B)
    mkv = mem_kv(mem, lw)
    o_m = mem_attend(p['q_m'], mkv)
    y = merge(x, o_a, o_b, o_m, p, lw)
    win_state = p['rows_win'][:, T - min(WINDOW, T):]
    return y, p['rows_a'], p['rows_cmp'], p['rows_slc'], win_state, mkv


def sample_layer(x, l, cache_mla, cache_nsa_cmp, cache_nsa_slc, win_buf, mem_kv_c, page_table, lw, rel_bias):
    DB, S, _ = x.shape
    pos = PAST_LEN + jnp.arange(S)
    p = prep(x, pos, lw)
    L = PAST_LEN + S
    k_pos_all = jnp.arange(L)
    n_sel = -(-L // SEL_BLOCK)
    n_past_blk = PAST_LEN // SEL_BLOCK
    n_new_blk = n_sel - n_past_blk
    bpp = PAGE_SIZE // SEL_BLOCK
    wb = min(WINDOW, PAST_LEN)
    w_pos = jnp.concatenate([PAST_LEN - wb + jnp.arange(wb), pos])
    kvi = jnp.arange(2)
    g6 = jnp.arange(NSA_KV_GROUPS)[None, None, :, None, None, None]
    row_in = jnp.arange(SEL_BLOCK)

    def one(args):
        q_a, rows_a, q_b, g_b, r_cmp, r_slc, r_win, buf, pt = args
        past_a = cache_mla[l, pt].reshape(PAST_LEN, rows_a.shape[-1])
        k, v = mla_kv_from_rows(jnp.concatenate([past_a, rows_a], 0)[None], lw)
        o_a = mla_attend(q_a[None], pos, k, v, k_pos_all)[0]
        past_c = cache_nsa_cmp[l, pt].reshape((PAST_LEN,) + r_cmp.shape[1:])
        kc, vc, c_end = nsa_compress(jnp.concatenate([past_c, r_cmp], 0)[None], lw)
        new_blocks = jnp.pad(r_slc, ((0, n_new_blk * SEL_BLOCK - S), (0, 0), (0, 0), (0, 0)))

        def fetch(idx):
            i5 = idx[..., None]
            page = pt[jnp.minimum(i5, n_past_blk - 1) // bpp]
            off = (i5 % bpp) * SEL_BLOCK + row_in
            past_blk = cache_nsa_slc[l, page[..., None], off[..., None], kvi, g6]
            nrow = jnp.clip(i5 - n_past_blk, 0, n_new_blk - 1) * SEL_BLOCK + row_in
            new_blk = new_blocks[nrow[..., None], kvi, g6]
            blk = jnp.where((i5 < n_past_blk)[..., None, None], past_blk, new_blk)
            return blk[..., 0, :], blk[..., 1, :]

        kw = jnp.concatenate([buf, r_win], 0)
        o_b = nsa_core(q_b[None], pos, kc, vc, c_end, fetch, n_sel, kw[None, :, 0], kw[None, :, 1],
                       w_pos, g_b[None], rel_bias)[0]
        return o_a, o_b, kw[S:]

    o_a, o_b, new_buf = lax.map(one, (p['q_a'], p['rows_a'], p['q_b'], p['g_b'], p['rows_cmp'],
                                      p['rows_slc'], p['rows_win'], win_buf, page_table))
    o_m = mem_attend(p['q_m'], mem_kv_c)
    y = merge(x, o_a.reshape(DB, S, MIX_A), o_b, o_m, p, lw)
    return y, p['rows_a'], p['rows_cmp'], p['rows_slc'], new_buf


def setup_inputs(seed: int = 0) -> dict:
    key = jax.random.key(seed)
    ks = jax.random.split(key, 27)
    n_pages = PAST_LEN // PAGE_SIZE
    n_pool = DEC_BATCH * n_pages + (DEC_BATCH * n_pages + 3) // 4
    wb = min(WINDOW, PAST_LEN)
    G = NSA_KV_GROUPS
    nrm = lambda k, shape, s=1.0: s * jax.random.normal(k, shape, jnp.float32)
    gain = lambda k, shape: 1.0 + 0.05 * jax.random.normal(k, shape, jnp.float32)
    page_table = jax.random.permutation(ks[8], n_pool)[:DEC_BATCH * n_pages].reshape(DEC_BATCH, n_pages).astype(jnp.int32)
    return {
        "x_prompt": nrm(ks[0], (BATCH, SEQ, D_MODEL)),
        "x_sample": nrm(ks[1], (DEC_BATCH, DEC_SEQ, D_MODEL)),
        "mem_prompt": nrm(ks[2], (BATCH, MEM_TOKENS, D_MODEL)),
        "cache_mla": nrm(ks[3], (DEPTH, n_pool, PAGE_SIZE, MLA_KV_LORA + MLA_ROPE)),
        "cache_nsa_cmp": nrm(ks[4], (DEPTH, n_pool, PAGE_SIZE, 2, G, HEAD_DIM)),
        "cache_nsa_slc": nrm(ks[5], (DEPTH, n_pool, PAGE_SIZE, 2, G, HEAD_DIM)),
        "cache_nsa_win": nrm(ks[6], (DEPTH, DEC_BATCH, wb, 2, G, HEAD_DIM)),
        "cache_mem_kv": nrm(ks[7], (DEPTH, DEC_BATCH, MEM_TOKENS, 2, MEM_HEADS, HEAD_DIM)),
        "page_table": page_table,
        "norm_g": gain(ks[9], (DEPTH, D_MODEL)),
        "w_in": nrm(ks[10], (DEPTH, D_MODEL, IN_W), D_MODEL ** -0.5),
        "mla_q_norm": gain(ks[11], (DEPTH, MLA_Q_LORA)),
        "mla_w_uq": nrm(ks[12], (DEPTH, MLA_Q_LORA, MLA_HEADS * MLA_QK), MLA_Q_LORA ** -0.5),
        "mla_kv_norm": gain(ks[13], (DEPTH, MLA_KV_LORA)),
        "mla_w_ukv": nrm(ks[14], (DEPTH, MLA_KV_LORA, MLA_HEADS * (MLA_NOPE + MLA_V)), MLA_KV_LORA ** -0.5),
        "mla_nope_g": gain(ks[15], (DEPTH, 2, MLA_NOPE)),
        "mla_rope_g": gain(ks[16], (DEPTH, 2, MLA_ROPE)),
        "nsa_qk_g": gain(ks[17], (DEPTH, 4, HEAD_DIM)),
        "nsa_cmp_pe": nrm(ks[18], (DEPTH, 2, CMP_BLOCK, HEAD_DIM), 0.1),
        "nsa_cmp_w1": nrm(ks[19], (DEPTH, 2, CMP_BLOCK * HEAD_DIM, CMP_HID), (CMP_BLOCK * HEAD_DIM) ** -0.5),
        "nsa_cmp_b1": nrm(ks[20], (DEPTH, 2, CMP_HID), 0.01),
        "nsa_cmp_w2": nrm(ks[21], (DEPTH, 2, CMP_HID, HEAD_DIM), CMP_HID ** -0.5),
        "mem_norm_g": gain(ks[22], (DEPTH, D_MODEL)),
        "w_mem_kv": nrm(ks[23], (DEPTH, D_MODEL, 2 * MEM_HEADS * HEAD_DIM), D_MODEL ** -0.5),
        "mem_qk_g": gain(ks[24], (DEPTH, 2, HEAD_DIM)),
        "w_out": nrm(ks[25], (DEPTH, MIX, D_MODEL), MIX ** -0.5),
        "rel_bias": nrm(ks[26], (REL_BUCKETS, NSA_HEADS), 0.1),
    }


def reference(x_prompt, x_sample, mem_prompt, cache_mla, cache_nsa_cmp, cache_nsa_slc, cache_nsa_win,
              cache_mem_kv, page_table, norm_g, w_in, mla_q_norm, mla_w_uq, mla_kv_norm, mla_w_ukv,
              mla_nope_g, mla_rope_g, nsa_qk_g, nsa_cmp_pe, nsa_cmp_w1, nsa_cmp_b1, nsa_cmp_w2,
              mem_norm_g, w_mem_kv, mem_qk_g, w_out, rel_bias):
    yp, ys = x_prompt, x_sample
    mla_p, mla_s, cmp_p, cmp_s, slc_p, slc_s, win_p, win_s, memkv_p = ([] for _ in range(9))
    for l in range(DEPTH):
        lw = dict(norm_g=norm_g[l], w_in=w_in[l], mla_q_norm=mla_q_norm[l], mla_w_uq=mla_w_uq[l],
                  mla_kv_norm=mla_kv_norm[l], mla_w_ukv=mla_w_ukv[l], mla_nope_g=mla_nope_g[l],
                  mla_rope_g=mla_rope_g[l], nsa_qk_g=nsa_qk_g[l], nsa_cmp_pe=nsa_cmp_pe[l],
                  nsa_cmp_w1=nsa_cmp_w1[l], nsa_cmp_b1=nsa_cmp_b1[l], nsa_cmp_w2=nsa_cmp_w2[l],
                  mem_norm_g=mem_norm_g[l], w_mem_kv=w_mem_kv[l], mem_qk_g=mem_qk_g[l], w_out=w_out[l])
        yp, ra, rc, rs, rw, mk = prompt_layer(yp, mem_prompt, lw, rel_bias)
        mla_p.append(ra); cmp_p.append(rc); slc_p.append(rs); win_p.append(rw); memkv_p.append(mk)
        ys, sa, sc, ss, sw = sample_layer(ys, l, cache_mla, cache_nsa_cmp, cache_nsa_slc, cache_nsa_win[l],
                                          cache_mem_kv[l], page_table, lw, rel_bias)
        mla_s.append(sa); cmp_s.append(sc); slc_s.append(ss); win_s.append(sw)
    return (yp, ys, jnp.stack(mla_p), jnp.stack(mla_s), jnp.stack(cmp_p), jnp.stack(cmp_s),
            jnp.stack(slc_p), jnp.stack(slc_s), jnp.stack(win_p), jnp.stack(win_s), jnp.stack(memkv_p))
```

```python
import functools
import math

import numpy as np
import jax
import jax.numpy as jnp
from jax import lax
from jax.experimental import pallas as pl
from jax.experimental.pallas import tpu as pltpu

F32 = jnp.float32
BF16 = jnp.bfloat16

D_MODEL = 1024
PAGE = 128
HD = 64
MLA_H = 6
MLA_QL = 256
MLA_KVL = 128
MLA_NOPE = 64
MLA_ROPE = 32
MLA_QK = MLA_NOPE + MLA_ROPE
ROPE_THETA = 10000.0
NSA_H = 6
NSA_G = 2
NSA_HPG = 3
CMP_BLOCK = 32
CMP_STRIDE = 16
CMP_HID = 128
SEL_BLOCK = 64
TOP_N = 16
WINDOW = 512
FORCE_BONUS = 1000.0
MEM_H = 4
REL_BUCKETS = 32
REL_MAX_DIST = 128
EPS = 1e-6
NEG = -1e30
M_INIT = -1e29
PAD_IMP = -3e38

LANES = 128
TQ = 256
CPAD = 16
VMEM_LIMIT = 56 * 1024 * 1024

C_QL, C_KVL, C_KR, C_NQ, C_NKV, C_NG, C_MQ, C_GATE, W_IN_P = 0, 256, 384, 512, 896, 1664, 1792, 2048, 3072


def _cp(sem):
    return pltpu.CompilerParams(dimension_semantics=sem, vmem_limit_bytes=VMEM_LIMIT)


def _dot(a, b):
    return jnp.dot(a, b, preferred_element_type=F32)


def _dot_nt(a, b):
    return lax.dot_general(a, b, (((1,), (1,)), ((), ())), preferred_element_type=F32)


def _split_bf16(y):
    hi = y.astype(BF16)
    lo = (y - hi.astype(F32)).astype(BF16)
    return jnp.concatenate([hi, lo], axis=1)


def _seg_ms(y, s2):
    return _dot(_split_bf16(y * y), s2)


def _lane(shape):
    return lax.broadcasted_iota(jnp.int32, shape, len(shape) - 1)


def _row(shape):
    return lax.broadcasted_iota(jnp.int32, shape, len(shape) - 2)


def _rel_bucket_np(d):
    n = np.maximum(d, 0)
    exact = REL_BUCKETS // 2
    logp = np.log(np.maximum(n, 1).astype(np.float32) / np.float32(exact)) / np.float32(math.log(REL_MAX_DIST / exact))
    large = np.minimum(exact + (logp * np.float32(REL_BUCKETS - exact)).astype(np.int32), REL_BUCKETS - 1)
    b = np.where(n < exact, n, large)
    return np.where((d < 0) | (d >= REL_MAX_DIST), REL_BUCKETS - 1, b).astype(np.int32)


def _bucket_tables():
    i = np.arange(TQ)[:, None]
    j = np.arange(TQ)[None, :]
    diag = _rel_bucket_np(i - j)
    near = _rel_bucket_np(TQ + i - j)
    w = np.arange(TQ)[None, :]
    cmpn = _rel_bucket_np(np.where(w < 2 * CPAD, i + (TQ - CMP_BLOCK + 1) - CMP_STRIDE * w, -1))
    vec = _rel_bucket_np(np.broadcast_to(np.arange(TQ)[None, :], (8, TQ)))
    return np.concatenate([diag, near, cmpn, vec], axis=0)


def _seg_mats():
    s64 = np.kron(np.eye(2), np.ones((64, 64))) / 64.0
    sqa = np.zeros((128, 128))
    sqa[:64, :64] = 1.0 / 64.0
    sqa[64:96, 64:96] = 1.0 / 32.0
    out = np.stack([np.concatenate([s64, s64], 0), np.concatenate([sqa, sqa], 0)])
    return jnp.asarray(out, BF16)


def _seg_mat_heads(nh):
    s = np.kron(np.eye(nh), np.ones((64, 64))) / 64.0
    return jnp.asarray(np.concatenate([s, s], 0), BF16)


def _gate_expand():
    e = np.zeros((128, 3 * NSA_H * HD))
    for h in range(NSA_H):
        for b in range(3):
            e[h * 3 + b, b * NSA_H * HD + h * HD:b * NSA_H * HD + (h + 1) * HD] = 1.0
    return jnp.asarray(np.concatenate([e, e], 0), BF16)


def _cmp_to_sel(n_c, n_sel, n_sel_pad):
    c = np.arange(n_c)[:, None] * CMP_STRIDE
    j = np.arange(n_sel_pad)[None, :] * SEL_BLOCK
    m = ((c < j + SEL_BLOCK) & (c + CMP_BLOCK > j) & (np.arange(n_c)[:, None] < n_c - 1)
         & (np.arange(n_sel_pad)[None, :] < n_sel)).astype(np.float64)
    out = np.zeros((n_c + 2 * CPAD, n_sel_pad))
    out[CPAD:CPAD + n_c] = m
    return jnp.asarray(out, BF16)


def _bias_kernel(tbl_ref, bk_ref, o_ref):
    h = pl.program_id(0)
    bk = bk_ref[...]
    base = tbl_ref[REL_BUCKETS - 1, h]
    acc = jnp.zeros(bk.shape, F32)
    for b in range(REL_BUCKETS - 1):
        acc = jnp.where(bk == b, tbl_ref[b, h] - base, acc)
    o_ref[0] = acc


def _bias_tables(rel_bias):
    bk = jnp.asarray(_bucket_tables())
    rows = bk.shape[0]
    return pl.pallas_call(
        _bias_kernel,
        grid=(NSA_H,),
        in_specs=[pl.BlockSpec(memory_space=pltpu.SMEM),
                  pl.BlockSpec((rows, TQ), lambda h: (0, 0))],
        out_specs=pl.BlockSpec((1, rows, TQ), lambda h: (h, 0, 0)),
        out_shape=jax.ShapeDtypeStruct((NSA_H, rows, TQ), F32),
        compiler_params=_cp(("arbitrary",)),
        name="bias_tables",
    )(rel_bias, bk)


def _rope(y, cos, sin):
    lane = _lane(y.shape)
    rot = jnp.where(lane < MLA_NOPE + MLA_ROPE // 2, pltpu.roll(y, LANES - MLA_ROPE // 2, 1),
                    pltpu.roll(y, MLA_ROPE // 2, 1))
    return y * cos + rot * sin


def _four_variants(y):
    lo = _lane(y.shape) < HD
    r = pltpu.roll(y, HD, 1)
    z = jnp.zeros_like(y)
    return [jnp.where(lo, y, z), jnp.where(lo, z, r), jnp.where(lo, r, z), jnp.where(lo, z, y)]


def _prep_kernel(x_ref, cos_ref, sin_ref, gv_ref, win_ref, wuq_ref, wuk_ref, wuv_ref, s2_ref,
                 qa_ref, rowsa_ref, ka_ref, va_ref, qb_ref, cmp_ref, slc_ref, wn_ref, gb_ref, qm_ref,
                 gates_ref, ks4_ref, vs4_ref, kw4_ref, vw4_ref):
    x = x_ref[...]
    cos = cos_ref[...]
    sin = sin_ref[...]
    s64 = s2_ref[0]
    sqa = s2_ref[1]

    def gv(i, w=LANES):
        return gv_ref[i:i + 1, 0:w]

    xn = (x * lax.rsqrt(jnp.mean(x * x, axis=-1, keepdims=True) + EPS) * gv(0, D_MODEL)).astype(BF16)

    def proj(a, b):
        return _dot(xn, win_ref[:, a:b])

    ql = proj(C_QL, C_QL + MLA_QL)
    qln = (ql * lax.rsqrt(jnp.mean(ql * ql, axis=-1, keepdims=True) + EPS) * gv(1, MLA_QL)).astype(BF16)
    q = _dot(qln, wuq_ref[...])
    for h in range(MLA_H):
        y = q[:, h * LANES:(h + 1) * LANES]
        yn = y * lax.rsqrt(_seg_ms(y, sqa) + EPS) * gv(3)
        qa_ref[:, h * LANES:(h + 1) * LANES] = _rope(yn, cos, sin).astype(BF16)
    kvl = proj(C_KVL, C_KVL + MLA_KVL)
    cn = kvl * lax.rsqrt(jnp.mean(kvl * kvl, axis=-1, keepdims=True) + EPS) * gv(2)
    krs = proj(C_KR, C_KR + LANES)
    krr = _rope(krs * lax.rsqrt(_seg_ms(krs, sqa) + EPS) * gv(4), cos, sin)
    rowsa_ref[:, 0:MLA_KVL] = cn
    rowsa_ref[:, MLA_KVL:MLA_KVL + MLA_ROPE] = krr[:, MLA_NOPE:MLA_NOPE + MLA_ROPE]
    cb = cn.astype(BF16)
    kraw = _dot(cb, wuk_ref[...])
    for h in range(MLA_H):
        y = kraw[:, h * LANES:(h + 1) * LANES]
        kn = y * lax.rsqrt(_seg_ms(y, s64) + EPS) * gv(5)
        ka_ref[:, h * LANES:(h + 1) * LANES] = (kn + krr).astype(BF16)
    va_ref[...] = _dot(cb, wuv_ref[...]).astype(BF16)
    nq = proj(C_NQ, C_NQ + NSA_H * HD)
    for v in range(NSA_H * HD // LANES):
        y = nq[:, v * LANES:(v + 1) * LANES]
        qb_ref[:, v * LANES:(v + 1) * LANES] = (y * lax.rsqrt(_seg_ms(y, s64) + EPS) * gv(6)).astype(BF16)
    nkv = proj(C_NKV, C_NKV + 6 * LANES)
    cmp_ref[...] = nkv[:, 0:2 * LANES]
    for base, gi, rows_ref, k4_ref, v4_ref in ((2 * LANES, 7, slc_ref, ks4_ref, vs4_ref),
                                               (4 * LANES, 8, wn_ref, kw4_ref, vw4_ref)):
        k = nkv[:, base:base + LANES]
        kn = k * lax.rsqrt(_seg_ms(k, s64) + EPS) * gv(gi)
        vv = nkv[:, base + LANES:base + 2 * LANES]
        rows_ref[:, 0:LANES] = kn
        rows_ref[:, LANES:2 * LANES] = vv
        for i, (a, b) in enumerate(zip(_four_variants(kn), _four_variants(vv))):
            k4_ref[:, i * LANES:(i + 1) * LANES] = a.astype(BF16)
            v4_ref[:, i * LANES:(i + 1) * LANES] = b.astype(BF16)
    ng = proj(C_NG, C_NG + LANES)
    gb_ref[...] = 1.0 / (1.0 + jnp.exp(-ng))
    mq = proj(C_MQ, C_MQ + MEM_H * HD)
    for v in range(MEM_H * HD // LANES):
        y = mq[:, v * LANES:(v + 1) * LANES]
        qm_ref[:, v * LANES:(v + 1) * LANES] = (y * lax.rsqrt(_seg_ms(y, s64) + EPS) * gv(9)).astype(BF16)
    gt = proj(C_GATE, W_IN_P)
    gates_ref[...] = gt / (1.0 + jnp.exp(-gt))


_PREP_OUT = (("qa", 768, BF16), ("rows_a", 160, F32), ("ka", 768, BF16), ("va", 384, BF16), ("qb", 384, BF16),
             ("rows_cmp", 256, F32), ("rows_slc", 256, F32), ("rows_win", 256, F32), ("gb", 128, F32),
             ("qm", 256, BF16), ("gates", 1024, F32), ("ks4", 512, BF16), ("vs4", 512, BF16),
             ("kw4", 512, BF16), ("vw4", 512, BF16))


def _prep(x2, cos, sin, lw, tm):
    R = x2.shape[0]
    n_tab = cos.shape[0] // tm
    full = lambda a: pl.BlockSpec(a.shape, lambda i: (0,) * a.ndim)
    outs = pl.pallas_call(
        _prep_kernel,
        grid=(R // tm,),
        in_specs=[pl.BlockSpec((tm, D_MODEL), lambda i: (i, 0)),
                  pl.BlockSpec((tm, LANES), lambda i: (i % n_tab, 0)),
                  pl.BlockSpec((tm, LANES), lambda i: (i % n_tab, 0)),
                  full(lw["gv"]), full(lw["w_in"]), full(lw["w_uq"]), full(lw["w_uk_p"]), full(lw["w_uv"]),
                  full(lw["s2"])],
        out_specs=[pl.BlockSpec((tm, w), lambda i: (i, 0)) for _, w, _ in _PREP_OUT],
        out_shape=[jax.ShapeDtypeStruct((R, w), dt) for _, w, dt in _PREP_OUT],
        compiler_params=_cp(("arbitrary",)),
        name="prep",
    )(x2, cos, sin, lw["gv"], lw["w_in"], lw["w_uq"], lw["w_uk_p"], lw["w_uv"], lw["s2"])
    return {name: o for (name, _, _), o in zip(_PREP_OUT, outs)}


def _flash_kernel(*refs, mode):
    if mode == "mla":
        q_ref, k_ref, v_ref, o_ref, m_scr, l_scr, acc_scr = refs
        sel_ref = bd_ref = None
    elif mode == "slc":
        q_ref, sel_ref, k_ref, v_ref, bd_ref, o_ref, m_scr, l_scr, acc_scr = refs
    else:
        q_ref, k_ref, v_ref, bd_ref, o_ref, m_scr, l_scr, acc_scr = refs
        sel_ref = None
    qi = pl.program_id(1)
    t = TQ
    m_scr[...] = jnp.full(m_scr.shape, M_INIT, F32)
    l_scr[...] = jnp.zeros(l_scr.shape, F32)
    acc_scr[...] = jnp.zeros(acc_scr.shape, F32)

    def tile(j, kind):
        rows = pl.ds(pl.multiple_of(j * t, t), t)
        if mode == "slc":
            blk = (j * t + _row((t, LANES))) // SEL_BLOCK
            onehot = (_lane((t, LANES)) == blk).astype(BF16)
        for h in range(NSA_H):
            pair = (h // 2) * LANES
            if mode == "mla":
                qh = q_ref[0, :, h * LANES:(h + 1) * LANES]
                kh = k_ref[0, rows, h * LANES:(h + 1) * LANES]
                vh = v_ref[0, rows, pair:pair + LANES]
            else:
                gp = ((h // NSA_HPG) * 2 + h % 2) * LANES
                qh = q_ref[0, :, pair:pair + LANES]
                kh = k_ref[0, rows, gp:gp + LANES]
                vh = v_ref[0, rows, gp:gp + LANES]
                if mode == "slc":
                    g = h // NSA_HPG
                    qh = jnp.concatenate([qh, sel_ref[0, :, g * LANES:(g + 1) * LANES]], axis=1)
                    kh = jnp.concatenate([kh, onehot], axis=1)
            s = _dot_nt(qh, kh)
            if kind == "near" and mode != "mla":
                s = s + bd_ref[h, 1]
            if kind == "diag":
                if mode != "mla":
                    s = s + bd_ref[h, 0]
                s = jnp.where(_row((t, t)) >= _lane((t, t)), s, NEG)
            if kind == "wfar":
                s = jnp.where(_lane((t, t)) > _row((t, t)), s, NEG)
            m_prev = m_scr[h]
            m_new = jnp.maximum(m_prev, jnp.max(s, axis=1, keepdims=True))
            alpha = jnp.exp(m_prev - m_new)
            p = jnp.exp(s - m_new[:, 0:1])
            l_scr[h] = alpha * l_scr[h] + jnp.sum(p, axis=1, keepdims=True)
            acc_scr[h] = alpha * acc_scr[h] + _dot(p.astype(BF16), vh)
            m_scr[h] = m_new

    if mode == "mla":
        def far(j, c):
            tile(j, "far")
            return c
        lax.fori_loop(0, qi, far, 0)
    elif mode == "slc":
        def far(j, c):
            tile(j, "far")
            return c
        lax.fori_loop(0, jnp.maximum(qi - 1, 0), far, 0)
    else:
        @pl.when(qi >= 2)
        def _():
            tile(qi - 2, "wfar")
    if mode != "mla":
        @pl.when(qi >= 1)
        def _():
            tile(qi - 1, "near")
    tile(qi, "diag")

    lo = _lane((t, LANES)) < HD
    for pp in range(NSA_H // 2):
        ra = acc_scr[2 * pp] / jnp.maximum(l_scr[2 * pp], 1e-30)
        rb = acc_scr[2 * pp + 1] / jnp.maximum(l_scr[2 * pp + 1], 1e-30)
        o_ref[0, :, pp * LANES:(pp + 1) * LANES] = jnp.where(lo, ra, rb) if mode == "mla" else ra + rb


def _flash(mode, q, k, v, sel=None, bd=None):
    N, T, wq = q.shape
    nq = T // TQ
    whole = lambda a: pl.BlockSpec((1,) + a.shape[1:], lambda n, i: (n, 0, 0), pipeline_mode=pl.Buffered(1))
    args = [q]
    specs = [pl.BlockSpec((1, TQ, wq), lambda n, i: (n, i, 0))]
    if mode == "slc":
        args.append(sel)
        specs.append(pl.BlockSpec((1, TQ, sel.shape[2]), lambda n, i: (n, i, 0)))
    args += [k, v]
    specs += [whole(k), whole(v)]
    if mode != "mla":
        args.append(bd)
        specs.append(pl.BlockSpec(bd.shape, lambda n, i: (0, 0, 0, 0), pipeline_mode=pl.Buffered(1)))
    return pl.pallas_call(
        functools.partial(_flash_kernel, mode=mode),
        grid=(N, nq),
        in_specs=specs,
        out_specs=pl.BlockSpec((1, TQ, NSA_H * HD), lambda n, i: (n, i, 0)),
        out_shape=jax.ShapeDtypeStruct((N, T, NSA_H * HD), F32),
        scratch_shapes=[pltpu.VMEM((NSA_H, TQ, LANES), F32)] * 3,
        compiler_params=_cp(("arbitrary", "arbitrary")),
        name="flash_" + mode,
    )(*args)


def _gather_pages(cache_ref, pt_ref, seq, buf, sem, slot, npg, start, halves=False):
    for j in range(npg):
        rows = pl.ds(j * PAGE, PAGE)
        if halves:
            cps = [pltpu.make_async_copy(cache_ref.at[pt_ref[seq, j], :, pl.ds(hf * LANES, LANES)],
                                         buf.at[slot, hf, rows], sem.at[slot]) for hf in range(2)]
        else:
            cps = [pltpu.make_async_copy(cache_ref.at[pt_ref[seq, j]], buf.at[slot, rows], sem.at[slot])]
        for cp in cps:
            if start:
                cp.start()
            else:
                cp.wait()


def _compress_kernel(pt_ref, cache_ref, w1_ref, pe_ref, b1_ref, w2_ref, gk_ref, kc4_ref, vc4_ref, buf, sem, *, npg):
    s = pl.program_id(0)
    ns = pl.num_programs(0)
    slot = s % 2
    L = npg * PAGE
    n_c = L // CMP_STRIDE

    @pl.when(s == 0)
    def _():
        _gather_pages(cache_ref, pt_ref, 0, buf, sem, 0, npg, True, halves=True)

    @pl.when(s + 1 < ns)
    def _():
        _gather_pages(cache_ref, pt_ref, s + 1, buf, sem, 1 - slot, npg, True, halves=True)

    _gather_pages(cache_ref, pt_ref, s, buf, sem, slot, npg, False, halves=True)
    for hf in range(2):
        buf[slot, hf, pl.ds(L, CMP_STRIDE), :] = jnp.zeros((CMP_STRIDE, LANES), F32)

    zeros16 = jnp.zeros((CPAD, 4 * LANES), BF16)
    kc4_ref[0, 0:CPAD, :] = zeros16
    kc4_ref[0, CPAD + n_c:2 * CPAD + n_c, :] = zeros16
    vc4_ref[0, 0:CPAD, :] = zeros16
    vc4_ref[0, CPAD + n_c:2 * CPAD + n_c, :] = zeros16

    rc = min(n_c, 256)
    npos = 4
    for c0 in range(0, n_c, rc):
        acc = [jnp.zeros((rc, CMP_HID), F32) for _ in range(4)]
        for p0 in range(0, CMP_BLOCK, npos):
            xs = [[buf[slot, hf, pl.ds(c0 * CMP_STRIDE + p0 + i, rc, stride=CMP_STRIDE), :] for i in range(npos)]
                  for hf in range(2)]
            for sg in range(4):
                lhs = jnp.concatenate([x[:, (sg % 2) * HD:(sg % 2 + 1) * HD] for x in xs[sg // 2]],
                                      axis=1).astype(BF16)
                acc[sg] = acc[sg] + _dot(lhs, w1_ref[sg // 2, p0 * HD:(p0 + npos) * HD, :])
        outs = []
        for sg in range(4):
            sidx = sg // 2
            b = _dot(pe_ref[sidx], w1_ref[sidx])[0:1, :] + b1_ref[sidx][0:1, :]
            hcur = acc[sg] + b
            hcur = 0.5 * hcur * (1.0 + jnp.tanh(0.7978845608028654 * (hcur + 0.044715 * hcur * hcur * hcur)))
            outs.append(_dot(hcur.astype(BF16), w2_ref[sidx]))
        z = jnp.zeros((rc, HD), F32)
        kn = [o * lax.rsqrt(jnp.mean(o * o, axis=-1, keepdims=True) + EPS) * gk_ref[...] for o in outs[0:2]]
        rows = slice(CPAD + c0, CPAD + c0 + rc)
        for i, (g, p) in enumerate(((0, 0), (0, 1), (1, 0), (1, 1))):
            kk = jnp.concatenate([kn[g], z] if p == 0 else [z, kn[g]], axis=1)
            vv = jnp.concatenate([outs[2 + g], z] if p == 0 else [z, outs[2 + g]], axis=1)
            kc4_ref[0, rows, i * LANES:(i + 1) * LANES] = kk.astype(BF16)
            vc4_ref[0, rows, i * LANES:(i + 1) * LANES] = vv.astype(BF16)


def _compress(pt, cache, lw, npg):
    n_seq = pt.shape[0]
    L = npg * PAGE
    n_c = L // CMP_STRIDE
    rows = n_c + 2 * CPAD
    full = lambda a: pl.BlockSpec(a.shape, lambda s, pt_: (0,) * a.ndim)
    ws = [lw["cmp_w1"], lw["cmp_pe"], lw["cmp_b1"], lw["cmp_w2"], lw["cmp_gk"]]
    return pl.pallas_call(
        functools.partial(_compress_kernel, npg=npg),
        grid_spec=pltpu.PrefetchScalarGridSpec(
            num_scalar_prefetch=1,
            grid=(n_seq,),
            in_specs=[pl.BlockSpec(memory_space=pl.ANY)] + [full(a) for a in ws],
            out_specs=[pl.BlockSpec((1, rows, 4 * LANES), lambda s, pt_: (s, 0, 0))] * 2,
            scratch_shapes=[pltpu.VMEM((2, 2, L + CMP_STRIDE, LANES), F32), pltpu.SemaphoreType.DMA((2,))]),
        out_shape=[jax.ShapeDtypeStruct((n_seq, rows, 4 * LANES), BF16)] * 2,
        compiler_params=_cp(("arbitrary",)),
        name="compress",
    )(pt, cache, *ws)


def _cmp_topk_kernel(q_ref, kc4_ref, vc4_ref, mm_ref, dcn_ref, oc_ref, sel_ref, idx_ref, *, tq, qpos0, n_c, n_sel):
    i = pl.program_id(1)
    q0 = qpos0 + i * tq
    c0p = pl.multiple_of((q0 // TQ) * CPAD, CPAD)
    nsp = mm_ref.shape[1]
    qpos = q0 + _row((tq, 1))
    wn = 2 * CPAD
    colf = _lane((tq, n_c))
    valid_far = colf < (c0p - CPAD)
    c_near = c0p - CPAD + _lane((tq, wn))
    valid_near = (c_near >= 0) & (c_near * CMP_STRIDE + (CMP_BLOCK - 1) <= qpos)
    far_rows = slice(CPAD, CPAD + n_c)
    near_rows = pl.ds(c0p, wn)
    idx_acc = jnp.zeros((tq, LANES), jnp.int32)
    lane_i = _lane((tq, LANES))
    res = [None] * NSA_H
    for g in range(NSA_G):
        ps_far = jnp.zeros((tq, n_c), F32)
        ps_near = jnp.zeros((tq, wn), F32)
        for hp in range(NSA_HPG):
            h = g * NSA_HPG + hp
            gp = (g * 2 + h % 2) * LANES
            qh = q_ref[0, :, (h // 2) * LANES:(h // 2 + 1) * LANES]
            s_far = jnp.where(valid_far, _dot_nt(qh, kc4_ref[0, far_rows, gp:gp + LANES]), NEG)
            s_near = _dot_nt(qh, kc4_ref[0, near_rows, gp:gp + LANES]) + dcn_ref[h][:, 0:wn]
            s_near = jnp.where(valid_near, s_near, NEG)
            m = jnp.maximum(jnp.maximum(jnp.max(s_far, axis=1, keepdims=True), jnp.max(s_near, axis=1, keepdims=True)),
                            M_INIT)
            pf = jnp.exp(s_far - m)
            pn = jnp.exp(s_near - m)
            inv = 1.0 / jnp.maximum(jnp.sum(pf, axis=1, keepdims=True) + jnp.sum(pn, axis=1, keepdims=True), 1e-30)
            pf = pf * inv
            pn = pn * inv
            res[h] = (_dot(pf.astype(BF16), vc4_ref[0, far_rows, gp:gp + LANES])
                      + _dot(pn.astype(BF16), vc4_ref[0, near_rows, gp:gp + LANES]))
            ps_far = ps_far + pf
            ps_near = ps_near + pn
        mf = mm_ref[far_rows, :]
        mn = mm_ref[near_rows, :]
        imp = (_dot(_split_bf16(ps_far), jnp.concatenate([mf, mf], axis=0))
               + _dot(_split_bf16(ps_near), jnp.concatenate([mn, mn], axis=0)))
        jl = _lane((tq, nsp))
        cur = qpos // SEL_BLOCK
        forced = (jl == 0) | (jl == cur) | (jl == cur - 1)
        valid = jl * SEL_BLOCK <= qpos
        imp = jnp.where(valid, imp + jnp.where(forced, FORCE_BONUS, 0.0), NEG)
        imp = jnp.where(jl < n_sel, imp, PAD_IMP)
        jf = jl.astype(F32)
        sel = jnp.zeros((tq, nsp), jnp.bool_)
        for kk in range(TOP_N):
            mx = jnp.max(imp, axis=1, keepdims=True)
            am = jnp.min(jnp.where(imp == mx, jf, 1e9), axis=1, keepdims=True)
            hit = jf == am
            sel = sel | hit
            imp = jnp.where(hit, -jnp.inf, imp)
            idx_acc = jnp.where(lane_i == g * TOP_N + kk, am.astype(jnp.int32), idx_acc)
        sel_ref[0, :, g * nsp:(g + 1) * nsp] = jnp.where(sel, 0.0, NEG).astype(BF16)
    for pp in range(NSA_H // 2):
        oc_ref[0, :, pp * LANES:(pp + 1) * LANES] = res[2 * pp] + res[2 * pp + 1]
    idx_ref[0] = idx_acc


def _cmp_topk(qb, kc4, vc4, dcn, tq, qpos0, n_sel):
    n_seq, T, _ = qb.shape
    rows = kc4.shape[1]
    n_c = rows - 2 * CPAD
    nsp = -(-n_sel // LANES) * LANES
    mm = _cmp_to_sel(n_c, n_sel, nsp)
    whole = lambda a: pl.BlockSpec((1,) + a.shape[1:], lambda n, i: (n, 0, 0))
    return pl.pallas_call(
        functools.partial(_cmp_topk_kernel, tq=tq, qpos0=qpos0, n_c=n_c, n_sel=n_sel),
        grid=(n_seq, T // tq),
        in_specs=[pl.BlockSpec((1, tq, NSA_H * HD), lambda n, i: (n, i, 0)), whole(kc4), whole(vc4),
                  pl.BlockSpec(mm.shape, lambda n, i: (0, 0)),
                  pl.BlockSpec(dcn.shape, lambda n, i: (0, 0, 0))],
        out_specs=[pl.BlockSpec((1, tq, NSA_H * HD), lambda n, i: (n, i, 0)),
                   pl.BlockSpec((1, tq, NSA_G * nsp), lambda n, i: (n, i, 0)),
                   pl.BlockSpec((1, tq, LANES), lambda n, i: (n, i, 0))],
        out_shape=[jax.ShapeDtypeStruct((n_seq, T, NSA_H * HD), F32),
                   jax.ShapeDtypeStruct((n_seq, T, NSA_G * nsp), BF16),
                   jax.ShapeDtypeStruct((n_seq, T, LANES), jnp.int32)],
        compiler_params=_cp(("arbitrary", "arbitrary")),
        name="cmp_topk",
    )(qb, kc4, vc4, mm, dcn)


def _mem_kv_kernel(mem_ref, g_ref, w_ref, s2_ref, gk_ref, o_ref):
    x = mem_ref[0]
    xn = (x * lax.rsqrt(jnp.mean(x * x, axis=-1, keepdims=True) + EPS) * g_ref[...]).astype(BF16)
    kv = _dot(xn, w_ref[...])
    nk = MEM_H * HD
    for v in range(nk // LANES):
        y = kv[:, v * LANES:(v + 1) * LANES]
        o_ref[0, :, v * LANES:(v + 1) * LANES] = y * lax.rsqrt(_seg_ms(y, s2_ref[0]) + EPS) * gk_ref[...]
    o_ref[0, :, nk:2 * nk] = kv[:, nk:2 * nk]


def _mem_kv(mem, lw):
    N, M, _ = mem.shape
    full = lambda a: pl.BlockSpec(a.shape, lambda n: (0,) * a.ndim)
    ws = [lw["mem_norm_g"], lw["w_mem_kv"], lw["s2"], lw["mem_gk"]]
    return pl.pallas_call(
        _mem_kv_kernel,
        grid=(N,),
        in_specs=[pl.BlockSpec((1, M, D_MODEL), lambda n: (n, 0, 0))] + [full(a) for a in ws],
        out_specs=pl.BlockSpec((1, M, 2 * MEM_H * HD), lambda n: (n, 0, 0)),
        out_shape=jax.ShapeDtypeStruct((N, M, 2 * MEM_H * HD), F32),
        compiler_params=_cp(("arbitrary",)),
        name="mem_kv",
    )(mem, *ws)


def _mem_attn_kernel(q_ref, kv_ref, o_ref):
    nk = MEM_H * HD
    tm = q_ref.shape[1]
    lo = _lane((kv_ref.shape[1], LANES)) < HD
    lo_o = _lane((tm, LANES)) < HD
    for pp in range(MEM_H // 2):
        qp = q_ref[0, :, pp * LANES:(pp + 1) * LANES]
        kp = kv_ref[0, :, pp * LANES:(pp + 1) * LANES]
        vp = kv_ref[0, :, nk + pp * LANES:nk + (pp + 1) * LANES].astype(BF16)
        r = []
        for par in range(2):
            kz = jnp.where(lo if par == 0 else ~lo, kp, 0.0).astype(BF16)
            s = _dot_nt(qp, kz)
            p = jnp.exp(s - jnp.max(s, axis=1, keepdims=True))
            r.append(_dot(p.astype(BF16), vp) / jnp.sum(p, axis=1, keepdims=True))
        o_ref[0, :, pp * LANES:(pp + 1) * LANES] = jnp.where(lo_o, r[0], r[1])


def _mem_attn(qm, mkv, tm):
    N, T, w = qm.shape
    return pl.pallas_call(
        _mem_attn_kernel,
        grid=(N, T // tm),
        in_specs=[pl.BlockSpec((1, tm, w), lambda n, i: (n, i, 0)),
                  pl.BlockSpec((1,) + mkv.shape[1:], lambda n, i: (n, 0, 0))],
        out_specs=pl.BlockSpec((1, tm, w), lambda n, i: (n, i, 0)),
        out_shape=jax.ShapeDtypeStruct((N, T, w), F32),
        compiler_params=_cp(("arbitrary", "arbitrary")),
        name="mem_attn",
    )(qm, mkv)


def _mla_decode_kernel(pt_ref, cache_ref, q6_ref, qr_ref, own_ref, wuk_ref, wuv_ref, s2_ref, gk_ref, o_ref,
                       buf, sem, *, npg, tk):
    s = pl.program_id(0)
    ns = pl.num_programs(0)
    slot = s % 2
    L = npg * PAGE

    @pl.when(s == 0)
    def _():
        _gather_pages(cache_ref, pt_ref, 0, buf, sem, 0, npg, True)

    @pl.when(s + 1 < ns)
    def _():
        _gather_pages(cache_ref, pt_ref, s + 1, buf, sem, 1 - slot, npg, True)

    _gather_pages(cache_ref, pt_ref, s, buf, sem, slot, npg, False)
    q6 = q6_ref[0]
    qr = qr_ref[0][:, 0:MLA_ROPE]
    gk = gk_ref[...]

    def chunk(c, kr, carry, valid=None):
        m_prev, l_prev, acc = carry
        cb = c.astype(BF16)
        kraw = _dot(cb, wuk_ref[...])
        kn = (kraw * lax.rsqrt(_seg_ms(kraw, s2_ref[...]) + EPS) * gk).astype(BF16)
        sc = _dot_nt(q6, kn) + _dot_nt(qr, kr.astype(BF16))
        if valid is not None:
            sc = jnp.where(valid, sc, NEG)
        m_new = jnp.maximum(m_prev, jnp.max(sc, axis=1, keepdims=True))
        alpha = jnp.exp(m_prev - m_new)
        p = jnp.exp(sc - m_new)
        return (m_new, alpha * l_prev + jnp.sum(p, axis=1, keepdims=True), alpha * acc + _dot(p.astype(BF16), cb))

    def body(i, carry):
        rows = pl.ds(pl.multiple_of(i * tk, tk), tk)
        return chunk(buf[slot, rows, 0:MLA_KVL], buf[slot, rows, MLA_KVL:MLA_KVL + MLA_ROPE], carry)

    carry = (jnp.full((8, 1), M_INIT, F32), jnp.zeros((8, 1), F32), jnp.zeros((8, MLA_KVL), F32))
    carry = lax.fori_loop(0, L // tk, body, carry)
    own = own_ref[0]
    _, l_fin, acc = chunk(own[:, 0:MLA_KVL], own[:, MLA_KVL:MLA_KVL + MLA_ROPE], carry, valid=_lane((8, 8)) == 0)
    o_lat = (acc / jnp.maximum(l_fin, 1e-30)).astype(BF16)
    res = _dot(o_lat, wuv_ref[...])
    w = MLA_H * HD
    own_head = (_lane((8, w)) // HD) == _row((8, w))
    o_ref[0] = jnp.broadcast_to(jnp.sum(jnp.where(own_head, res, 0.0), axis=0, keepdims=True), (8, w))


def _mla_decode(pt, cache, q6, qr, own, lw, npg):
    DB = pt.shape[0]
    L = npg * PAGE
    tk = min(512, L)
    full = lambda a: pl.BlockSpec(a.shape, lambda s, pt_: (0,) * a.ndim)
    per = lambda a: pl.BlockSpec((1,) + a.shape[1:], lambda s, pt_: (s, 0, 0))
    ws = [lw["w_uk"], lw["w_uv"], lw["s6"], lw["mla_gk"]]
    w = MLA_H * HD
    return pl.pallas_call(
        functools.partial(_mla_decode_kernel, npg=npg, tk=tk),
        grid_spec=pltpu.PrefetchScalarGridSpec(
            num_scalar_prefetch=1,
            grid=(DB,),
            in_specs=[pl.BlockSpec(memory_space=pl.ANY), per(q6), per(qr), per(own)] + [full(a) for a in ws],
            out_specs=pl.BlockSpec((1, 8, w), lambda s, pt_: (s, 0, 0)),
            scratch_shapes=[pltpu.VMEM((2, L, MLA_KVL + MLA_ROPE), F32), pltpu.SemaphoreType.DMA((2,))]),
        out_shape=jax.ShapeDtypeStruct((DB, 8, w), F32),
        compiler_params=_cp(("arbitrary",)),
        name="mla_decode",
    )(pt, cache, q6, qr, own, *ws)


def _decode_core(q8, keys, bias, valid, own_row, own_bias, own_flag):
    kb = keys.astype(BF16)
    s = _dot_nt(q8, kb) + bias
    if valid is not None:
        s = jnp.where(valid, s, NEG)
    s_own = jnp.sum(q8.astype(F32) * own_row, axis=1, keepdims=True) + own_bias
    s_own = jnp.where(own_flag, s_own, NEG)
    m = jnp.maximum(jnp.maximum(jnp.max(s, axis=1, keepdims=True), s_own), M_INIT)
    p = jnp.exp(s - m)
    p_own = jnp.exp(s_own - m)
    l = jnp.sum(p, axis=1, keepdims=True) + p_own
    return (_dot(p.astype(BF16), kb) + p_own * own_row) / jnp.maximum(l, 1e-30)


def _place_heads(res):
    v0 = res[0][:, 2 * HD:4 * HD]
    v1 = res[1][:, 2 * HD:4 * HD]
    v0r = pltpu.roll(v0, HD, 1)
    v1r = pltpu.roll(v1, HD, 1)
    lo = _lane((1, LANES)) < HD
    row = jnp.concatenate([jnp.where(lo, v0[0:1], v0r[1:2]), jnp.where(lo, v0[2:3], v1[0:1]),
                           jnp.where(lo, v1r[1:2], v1[2:3])], axis=1)
    return jnp.broadcast_to(row, (8, NSA_H * HD))


def _slc_decode_kernel(idx_ref, pt_ref, cache_ref, q8_ref, own_ref, bt_ref, d0_ref, o_ref, buf, sem, *, n_past_blk):
    s = pl.program_id(0)
    ns = pl.num_programs(0)
    slot = s % 2
    nblk = NSA_G * TOP_N
    bpp = PAGE // SEL_BLOCK

    def copies(seq, sl, start):
        for j in range(nblk):
            b = jnp.minimum(idx_ref[seq, j], n_past_blk - 1)
            src = pt_ref[seq, b // bpp] * bpp + b % bpp
            cp = pltpu.make_async_copy(cache_ref.at[src], buf.at[sl, pl.ds(j * SEL_BLOCK, SEL_BLOCK)], sem.at[sl])
            if start:
                cp.start()
            else:
                cp.wait()

    @pl.when(s == 0)
    def _():
        copies(0, 0, True)

    @pl.when(s + 1 < ns)
    def _():
        copies(s + 1, 1 - slot, True)

    copies(s, slot, False)
    own = own_ref[0][0:1, :]
    lo = _lane((8, LANES)) < HD
    nk = TOP_N * SEL_BLOCK
    res = []
    for g in range(NSA_G):
        pieces, vpieces = [], []
        own_flag = idx_ref[s, g * TOP_N] == n_past_blk
        for jj in range(TOP_N // 2):
            ia = idx_ref[s, g * TOP_N + 2 * jj]
            ib = idx_ref[s, g * TOP_N + 2 * jj + 1]
            own_flag = own_flag | (ia == n_past_blk) | (ib == n_past_blk)
            z = jnp.zeros((8, LANES), F32)
            bias = (jnp.where(ia == n_past_blk - 1, bt_ref[g, 0], z) + jnp.where(ia == n_past_blk - 2, bt_ref[g, 2], z)
                    + jnp.where(ib == n_past_blk - 1, bt_ref[g, 1], z) + jnp.where(ib == n_past_blk - 2, bt_ref[g, 3], z))
            pieces.append(bias)
            va = jnp.where(ia < n_past_blk, 1, 0)
            vb = jnp.where(ib < n_past_blk, 1, 0)
            vpieces.append(jnp.where(lo, va, vb) > 0)
        keys = buf[slot, g * nk:(g + 1) * nk, :]
        res.append(_decode_core(q8_ref[0, g], keys, jnp.concatenate(pieces, axis=1),
                                jnp.concatenate(vpieces, axis=1), own, d0_ref[g][:, 0:1], own_flag))
    o_ref[0] = _place_heads(res)


def _slc_decode(idx, pt, cache, q8, own, bt, d0, n_past_blk):
    DB = idx.shape[0]
    per3 = lambda a: pl.BlockSpec((1,) + a.shape[1:], lambda s, i_, p_: (s,) + (0,) * (a.ndim - 1))
    full = lambda a: pl.BlockSpec(a.shape, lambda s, i_, p_: (0,) * a.ndim)
    return pl.pallas_call(
        functools.partial(_slc_decode_kernel, n_past_blk=n_past_blk),
        grid_spec=pltpu.PrefetchScalarGridSpec(
            num_scalar_prefetch=2,
            grid=(DB,),
            in_specs=[pl.BlockSpec(memory_space=pl.ANY), per3(q8), per3(own), full(bt), full(d0)],
            out_specs=pl.BlockSpec((1, 8, NSA_H * HD), lambda s, i_, p_: (s, 0, 0)),
            scratch_shapes=[pltpu.VMEM((2, NSA_G * TOP_N * SEL_BLOCK, 2 * LANES), F32), pltpu.SemaphoreType.DMA((2,))]),
        out_shape=jax.ShapeDtypeStruct((DB, 8, NSA_H * HD), F32),
        compiler_params=_cp(("arbitrary",)),
        name="slc_decode",
    )(idx, pt, cache, q8, own, bt, d0)


def _win_decode_kernel(wb_ref, q8_ref, own_ref, bw_ref, d0_ref, o_ref, *, first_valid):
    keys = wb_ref[0]
    n = keys.shape[0]
    own = own_ref[0][0:1, :]
    valid = _lane((8, n)) >= first_valid
    o_ref[0] = _place_heads([_decode_core(q8_ref[0, g], keys, bw_ref[g], valid, own, d0_ref[g][:, 0:1], True)
                             for g in range(NSA_G)])


def _win_decode(wbuf, q8, own, bw, d0, first_valid):
    DB, wb, _ = wbuf.shape
    per = lambda a: pl.BlockSpec((1,) + a.shape[1:], lambda s: (s,) + (0,) * (a.ndim - 1))
    full = lambda a: pl.BlockSpec(a.shape, lambda s: (0,) * a.ndim)
    return pl.pallas_call(
        functools.partial(_win_decode_kernel, first_valid=first_valid),
        grid=(DB,),
        in_specs=[per(wbuf), per(q8), per(own), full(bw), full(d0)],
        out_specs=pl.BlockSpec((1, 8, NSA_H * HD), lambda s: (s, 0, 0)),
        out_shape=jax.ShapeDtypeStruct((DB, 8, NSA_H * HD), F32),
        compiler_params=_cp(("arbitrary",)),
        name="win_decode",
    )(wbuf, q8, own, bw, d0)


def _mem_decode_kernel(kv_ref, q8_ref, o_ref):
    kb = kv_ref[0].astype(BF16)
    s = _dot_nt(q8_ref[0], kb)
    p = jnp.exp(s - jnp.max(s, axis=1, keepdims=True))
    nk = MEM_H * HD
    res = (_dot(p.astype(BF16), kb) / jnp.sum(p, axis=1, keepdims=True))[:, nk:2 * nk]
    own_head = (_lane((8, nk)) // HD) == _row((8, nk))
    o_ref[0] = jnp.broadcast_to(jnp.sum(jnp.where(own_head, res, 0.0), axis=0, keepdims=True), (8, nk))


def _mem_decode(mkv, q8):
    DB, M, w = mkv.shape
    return pl.pallas_call(
        _mem_decode_kernel,
        grid=(DB,),
        in_specs=[pl.BlockSpec((1, M, w), lambda s: (s, 0, 0)), pl.BlockSpec((1, 8, w), lambda s: (s, 0, 0))],
        out_specs=pl.BlockSpec((1, 8, w // 2), lambda s: (s, 0, 0)),
        out_shape=jax.ShapeDtypeStruct((DB, 8, w // 2), F32),
        compiler_params=_cp(("arbitrary",)),
        name="mem_decode",
    )(mkv, q8)


def _merge_kernel(x_ref, oa_ref, oc_ref, os_ref, ow_ref, gb_ref, om_ref, gates_ref, e2_ref, wout_ref, y_ref):
    w = NSA_H * HD
    ge = _dot(_split_bf16(gb_ref[...]), e2_ref[...])
    ob = oc_ref[...] * ge[:, 0:w] + os_ref[...] * ge[:, w:2 * w] + ow_ref[...] * ge[:, 2 * w:3 * w]
    z = jnp.concatenate([oa_ref[...], ob, om_ref[...]], axis=1) * gates_ref[...]
    y_ref[...] = x_ref[...] + _dot(z.astype(BF16), wout_ref[...])


def _merge(x2, oa, oc, osl, ow, gb, om, gates, lw, tm):
    R = x2.shape[0]
    row = lambda a: pl.BlockSpec((tm, a.shape[1]), lambda i: (i, 0))
    full = lambda a: pl.BlockSpec(a.shape, lambda i: (0,) * a.ndim)
    acts = [x2, oa, oc, osl, ow, gb, om, gates]
    return pl.pallas_call(
        _merge_kernel,
        grid=(R // tm,),
        in_specs=[row(a) for a in acts] + [full(lw["e2"]), full(lw["w_out"])],
        out_specs=pl.BlockSpec((tm, D_MODEL), lambda i: (i, 0)),
        out_shape=jax.ShapeDtypeStruct((R, D_MODEL), F32),
        compiler_params=_cp(("arbitrary",)),
        name="merge",
    )(*acts, lw["e2"], lw["w_out"])


def _layer_weights(l, norm_g, w_in, mla_q_norm, mla_w_uq, mla_kv_norm, mla_w_ukv, mla_nope_g, mla_rope_g,
                   nsa_qk_g, nsa_cmp_pe, nsa_cmp_w1, nsa_cmp_b1, nsa_cmp_w2, mem_norm_g, w_mem_kv, mem_qk_g, w_out):
    w = w_in[l]
    o = np.cumsum((0, 256, 128, 32, 384, 384, 768, 18, 384, 256, 256))
    z = lambda n: jnp.zeros((D_MODEL, n), F32)
    w_in_p = jnp.concatenate([
        w[:, o[0]:o[1]], w[:, o[1]:o[2]],
        z(MLA_NOPE), w[:, o[2]:o[3]], z(LANES - MLA_NOPE - MLA_ROPE),
        w[:, o[4]:o[5]], w[:, o[5]:o[6]],
        w[:, o[6]:o[7]], z(LANES - 18),
        w[:, o[8]:o[9]],
        w[:, o[3]:o[4]], w[:, o[7]:o[8]], w[:, o[9]:o[10]]], axis=1).astype(BF16)
    uq = mla_w_uq[l].reshape(MLA_QL, MLA_H, MLA_QK)
    w_uq = jnp.pad(uq, ((0, 0), (0, 0), (0, LANES - MLA_QK))).reshape(MLA_QL, MLA_H * LANES).astype(BF16)
    ukv = mla_w_ukv[l].reshape(MLA_KVL, MLA_H, 2 * HD)
    w_uk = ukv[:, :, :HD].reshape(MLA_KVL, MLA_H * HD).astype(BF16)
    w_uk_p = jnp.pad(ukv[:, :, :HD], ((0, 0), (0, 0), (0, HD))).reshape(MLA_KVL, MLA_H * LANES).astype(BF16)
    w_uv = ukv[:, :, HD:].reshape(MLA_KVL, MLA_H * HD).astype(BF16)
    sa = MLA_QK ** -0.5
    sb = HD ** -0.5
    pad = lambda v: jnp.pad(v, (0, D_MODEL - v.shape[0]))
    z32 = jnp.zeros((32,), F32)
    z64 = jnp.zeros((64,), F32)
    qk = nsa_qk_g[l]
    gv = jnp.stack([
        norm_g[l], pad(mla_q_norm[l]), pad(mla_kv_norm[l]),
        pad(jnp.concatenate([mla_nope_g[l, 0] * sa, mla_rope_g[l, 0] * sa, z32])),
        pad(jnp.concatenate([z64, mla_rope_g[l, 1], z32])),
        pad(jnp.concatenate([mla_nope_g[l, 1], z64])),
        pad(jnp.tile(qk[0] * sb, 2)), pad(jnp.tile(qk[2], 2)), pad(jnp.tile(qk[3], 2)),
        pad(jnp.tile(mem_qk_g[l, 0] * sb, 2))] + [jnp.zeros((D_MODEL,), F32)] * 6)
    pe = nsa_cmp_pe[l].reshape(2, 1, CMP_BLOCK * HD)
    return dict(
        gv=gv, w_in=w_in_p, w_uq=w_uq, w_uk=w_uk, w_uk_p=w_uk_p, w_uv=w_uv, s2=_seg_mats(), s6=_seg_mat_heads(MLA_H),
        mla_gk=jnp.tile(mla_nope_g[l, 1], MLA_H)[None, :],
        cmp_w1=nsa_cmp_w1[l].astype(BF16), cmp_pe=jnp.broadcast_to(pe, (2, 8, CMP_BLOCK * HD)).astype(BF16),
        cmp_b1=jnp.broadcast_to(nsa_cmp_b1[l][:, None, :], (2, 8, CMP_HID)),
        cmp_w2=nsa_cmp_w2[l].astype(BF16), cmp_gk=qk[1][None, :],
        mem_norm_g=mem_norm_g[l][None, :], w_mem_kv=w_mem_kv[l].astype(BF16),
        mem_gk=jnp.tile(mem_qk_g[l, 1], 2)[None, :],
        e2=_gate_expand(), w_out=w_out[l].astype(BF16))


def _rope_tables(pos):
    half = MLA_ROPE // 2
    inv = ROPE_THETA ** (-jnp.arange(half, dtype=F32) / half)
    ang = pos.astype(F32)[:, None] * inv[None, :]
    c, s = jnp.cos(ang), jnp.sin(ang)
    n = pos.shape[0]
    cos = jnp.concatenate([jnp.ones((n, MLA_NOPE), F32), c, c, jnp.zeros((n, LANES - MLA_QK), F32)], axis=1)
    sin = jnp.concatenate([jnp.zeros((n, MLA_NOPE), F32), -s, s, jnp.zeros((n, LANES - MLA_QK), F32)], axis=1)
    return cos, sin


def _group_queries(qb):
    DB = qb.shape[0]
    qh = qb.reshape(DB, NSA_G, NSA_HPG, HD)
    out = jnp.zeros((DB, NSA_G, 8, 4, HD), qb.dtype)
    for g in range(NSA_G):
        out = out.at[:, g, :NSA_HPG, g].set(qh[:, g])
    return out.reshape(DB, NSA_G, 8, 4 * HD)


def kernel(x_prompt, x_sample, mem_prompt, cache_mla, cache_nsa_cmp, cache_nsa_slc, cache_nsa_win, cache_mem_kv,
           page_table, norm_g, w_in, mla_q_norm, mla_w_uq, mla_kv_norm, mla_w_ukv, mla_nope_g, mla_rope_g, nsa_qk_g,
           nsa_cmp_pe, nsa_cmp_w1, nsa_cmp_b1, nsa_cmp_w2, mem_norm_g, w_mem_kv, mem_qk_g, w_out, rel_bias):
    N, T, _ = x_prompt.shape
    DB = x_sample.shape[0]
    depth = norm_g.shape[0]
    n_pool = cache_mla.shape[1]
    npg = page_table.shape[1]
    past = npg * PAGE
    wb = cache_nsa_win.shape[2]
    assert x_sample.shape[1] == 1 and T % TQ == 0 and past % TQ == 0 and T // SEL_BLOCK <= LANES
    assert DB % 8 == 0 and wb == min(WINDOW, past) and wb >= REL_MAX_DIST and past // SEL_BLOCK >= 2
    npg_p = T // PAGE
    n_past_blk = past // SEL_BLOCK

    bias = _bias_tables(rel_bias)
    bd = jnp.stack([bias[:, 0:TQ], bias[:, TQ:2 * TQ]], axis=1)
    dcn = bias[:, 2 * TQ:3 * TQ, 0:LANES]
    dvec = bias[:, 3 * TQ, 0:REL_MAX_DIST + 1]
    dvec = dvec.at[:, REL_MAX_DIST].set(0.0)
    hg = jnp.arange(NSA_G)[:, None] * NSA_HPG + jnp.minimum(jnp.arange(8), NSA_HPG - 1)[None, :]
    dg = dvec[hg]
    r64 = jnp.arange(SEL_BLOCK)
    z64 = jnp.zeros((NSA_G, 8, SEL_BLOCK), F32)
    b1 = dg[:, :, SEL_BLOCK - r64]
    b2 = dg[:, :, 2 * SEL_BLOCK - r64]
    bt = jnp.stack([jnp.concatenate([b1, z64], -1), jnp.concatenate([z64, b1], -1),
                    jnp.concatenate([b2, z64], -1), jnp.concatenate([z64, b2], -1)], axis=1)
    d0 = jnp.broadcast_to(dg[:, :, 0:1], (NSA_G, 8, LANES))
    bw = dg[:, :, jnp.minimum(wb - jnp.arange(wb), REL_MAX_DIST)]

    cos_p, sin_p = _rope_tables(jnp.arange(T))
    cos_s, sin_s = _rope_tables(jnp.full((DB,), past))
    pt_prompt = jnp.arange(N * npg_p, dtype=jnp.int32).reshape(N, npg_p)
    tm_p = TQ

    yp = x_prompt.reshape(N * T, D_MODEL)
    ys = x_sample.reshape(DB, D_MODEL)
    outs = [[] for _ in range(9)]
    for l in range(depth):
        lw = _layer_weights(l, norm_g, w_in, mla_q_norm, mla_w_uq, mla_kv_norm, mla_w_ukv, mla_nope_g, mla_rope_g,
                            nsa_qk_g, nsa_cmp_pe, nsa_cmp_w1, nsa_cmp_b1, nsa_cmp_w2, mem_norm_g, w_mem_kv,
                            mem_qk_g, w_out)
        p = _prep(yp, cos_p, sin_p, lw, tm_p)
        r3 = lambda a: a.reshape(N, T, a.shape[-1])
        o_a = _flash("mla", r3(p["qa"]), r3(p["ka"]), r3(p["va"]))
        kc4, vc4 = _compress(pt_prompt, p["rows_cmp"].reshape(N * npg_p, PAGE, 2 * LANES), lw, npg_p)
        o_c, sel, _ = _cmp_topk(r3(p["qb"]), kc4, vc4, dcn, TQ, 0, T // SEL_BLOCK)
        o_s = _flash("slc", r3(p["qb"]), r3(p["ks4"]), r3(p["vs4"]), sel=sel, bd=bd)
        o_w = _flash("win", r3(p["qb"]), r3(p["kw4"]), r3(p["vw4"]), bd=bd)
        mkv = _mem_kv(mem_prompt, lw)
        o_m = _mem_attn(r3(p["qm"]), mkv, tm_p)
        f2 = lambda a: a.reshape(N * T, a.shape[-1])
        yp = _merge(yp, f2(o_a), f2(o_c), f2(o_s), f2(o_w), p["gb"], f2(o_m), p["gates"], lw, tm_p)
        outs[0].append(p["rows_a"].reshape(N, T, MLA_KVL + MLA_ROPE))
        outs[2].append(p["rows_cmp"].reshape(N, T, 2, NSA_G, HD))
        outs[4].append(p["rows_slc"].reshape(N, T, 2, NSA_G, HD))
        outs[6].append(p["rows_win"].reshape(N, T, 2, NSA_G, HD)[:, T - min(WINDOW, T):])
        outs[8].append(mkv.reshape(N, mkv.shape[1], 2, MEM_H, HD))
        ps = _prep(ys, cos_s, sin_s, lw, DB)
        pt_l = page_table + l * n_pool
        qa3 = ps["qa"].reshape(DB, MLA_H, LANES)
        q6 = (qa3[:, :, None, :MLA_NOPE] * jnp.eye(MLA_H, dtype=BF16)[None, :, :, None]).reshape(DB, MLA_H, MLA_H * HD)
        q6 = jnp.pad(q6, ((0, 0), (0, 8 - MLA_H), (0, 0)))
        qr = jnp.pad(qa3[:, :, MLA_NOPE:MLA_QK], ((0, 0), (0, 8 - MLA_H), (0, LANES - MLA_ROPE)))
        own8 = lambda a: jnp.pad(a[:, None, :], ((0, 0), (0, 7), (0, 0)))
        oa_s = _mla_decode(pt_l, cache_mla.reshape(depth * n_pool, PAGE, MLA_KVL + MLA_ROPE), q6, qr,
                           own8(ps["rows_a"]), lw, npg)[:, 0]
        kc4s, vc4s = _compress(pt_l, cache_nsa_cmp.reshape(depth * n_pool, PAGE, 2 * LANES), lw, npg)
        qb8 = jnp.pad(ps["qb"][:, None, :], ((0, 0), (0, 7), (0, 0)))
        oc_s, _, idx = _cmp_topk(qb8, kc4s, vc4s, dcn[:, 0:8], 8, past, n_past_blk + 1)
        q8 = _group_queries(ps["qb"])
        os_s = _slc_decode(idx[:, 0, 0:NSA_G * TOP_N], pt_l,
                           cache_nsa_slc.reshape(depth * n_pool * (PAGE // SEL_BLOCK), SEL_BLOCK, 2 * LANES),
                           q8, own8(ps["rows_slc"]), bt, d0, n_past_blk)
        ow_s = _win_decode(cache_nsa_win[l].reshape(DB, wb, 2 * LANES), q8, own8(ps["rows_win"]), bw, d0,
                           wb - WINDOW + 1)
        qm8 = jnp.zeros((DB, 8, 2 * MEM_H, HD), BF16)
        qmh = ps["qm"].reshape(DB, MEM_H, HD)
        for h in range(MEM_H):
            qm8 = qm8.at[:, h, h].set(qmh[:, h])
        om8 = _mem_decode(cache_mem_kv[l].reshape(DB, -1, 2 * MEM_H * HD), qm8.reshape(DB, 8, 2 * MEM_H * HD))
        ys = _merge(ys, oa_s, oc_s[:, 0], os_s[:, 0], ow_s[:, 0], ps["gb"], om8[:, 0], ps["gates"], lw, DB)
        outs[1].append(ps["rows_a"].reshape(DB, 1, MLA_KVL + MLA_ROPE))
        outs[3].append(ps["rows_cmp"].reshape(DB, 1, 2, NSA_G, HD))
        outs[5].append(ps["rows_slc"].reshape(DB, 1, 2, NSA_G, HD))
        new_win = jnp.concatenate([cache_nsa_win[l][:, 1:], ps["rows_win"].reshape(DB, 1, 2, NSA_G, HD)], axis=1)
        outs[7].append(new_win)
    return (yp.reshape(N, T, D_MODEL), ys.reshape(DB, 1, D_MODEL)) + tuple(jnp.stack(o) for o in outs)
```

```python
import functools
import math

import numpy as np
import jax
import jax.numpy as jnp
from jax import lax
from jax.experimental import pallas as pl
from jax.experimental.pallas import tpu as pltpu

F32 = jnp.float32
BF16 = jnp.bfloat16

D_MODEL = 1024
PAGE = 128
HD = 64
MLA_H = 6
MLA_QL = 256
MLA_KVL = 128
MLA_NOPE = 64
MLA_ROPE = 32
MLA_QK = MLA_NOPE + MLA_ROPE
ROPE_THETA = 10000.0
NSA_H = 6
NSA_G = 2
NSA_HPG = 3
CMP_BLOCK = 32
CMP_STRIDE = 16
CMP_HID = 128
SEL_BLOCK = 64
TOP_N = 16
WINDOW = 512
FORCE_BONUS = 1000.0
MEM_H = 4
REL_BUCKETS = 32
REL_MAX_DIST = 128
EPS = 1e-6
NEG = -1e30
M_INIT = -1e29
PAD_IMP = -3e38

LANES = 128
TQ = 256
TF = 512
CPAD = 16
VMEM_LIMIT = 56 * 1024 * 1024

C_QL, C_KVL, C_KR, C_NQ, C_NKV, C_NG, C_MQ, C_GATE, W_IN_P = 0, 256, 384, 512, 896, 1664, 1792, 2048, 3072


def _cp(sem):
    return pltpu.CompilerParams(dimension_semantics=sem, vmem_limit_bytes=VMEM_LIMIT)


def _dot(a, b):
    return jnp.dot(a, b, preferred_element_type=F32)


def _dot_nt(a, b):
    return lax.dot_general(a, b, (((1,), (1,)), ((), ())), preferred_element_type=F32)


def _split_bf16(y):
    hi = y.astype(BF16)
    lo = (y - hi.astype(F32)).astype(BF16)
    return jnp.concatenate([hi, lo], axis=1)


def _seg_ms(y, s2):
    return _dot(_split_bf16(y * y), s2)


def _lane(shape):
    return lax.broadcasted_iota(jnp.int32, shape, len(shape) - 1)


def _row(shape):
    return lax.broadcasted_iota(jnp.int32, shape, len(shape) - 2)


def _rel_bucket_np(d):
    n = np.maximum(d, 0)
    exact = REL_BUCKETS // 2
    logp = np.log(np.maximum(n, 1).astype(np.float32) / np.float32(exact)) / np.float32(math.log(REL_MAX_DIST / exact))
    large = np.minimum(exact + (logp * np.float32(REL_BUCKETS - exact)).astype(np.int32), REL_BUCKETS - 1)
    b = np.where(n < exact, n, large)
    return np.where((d < 0) | (d >= REL_MAX_DIST), REL_BUCKETS - 1, b).astype(np.int32)


def _bucket_tables(tf):
    i = np.arange(tf)[:, None]
    j = np.arange(tf)[None, :]
    diag = _rel_bucket_np(i - j)
    near = _rel_bucket_np(tf + i - j)
    i = np.arange(TQ)[:, None]
    cmpn = _rel_bucket_np(np.where(j < 2 * CPAD, i + (TQ - CMP_BLOCK + 1) - CMP_STRIDE * j, -1))
    vec = _rel_bucket_np(np.broadcast_to(j, (8, tf)))
    return np.concatenate([diag, near, cmpn, vec], axis=0)


def _seg_mats():
    s64 = np.kron(np.eye(2), np.ones((64, 64))) / 64.0
    sqa = np.zeros((128, 128))
    sqa[:64, :64] = 1.0 / 64.0
    sqa[64:96, 64:96] = 1.0 / 32.0
    out = np.stack([np.concatenate([s64, s64], 0), np.concatenate([sqa, sqa], 0)])
    return jnp.asarray(out, BF16)


def _seg_mat_heads(nh):
    s = np.kron(np.eye(nh), np.ones((64, 64))) / 64.0
    return jnp.asarray(np.concatenate([s, s], 0), BF16)


def _gate_expand():
    e = np.zeros((128, 3 * NSA_H * HD))
    for h in range(NSA_H):
        for b in range(3):
            e[h * 3 + b, b * NSA_H * HD + h * HD:b * NSA_H * HD + (h + 1) * HD] = 1.0
    return jnp.asarray(np.concatenate([e, e], 0), BF16)


def _cmp_to_sel(n_c, n_sel, n_sel_pad):
    c = np.arange(n_c)[:, None] * CMP_STRIDE
    j = np.arange(n_sel_pad)[None, :] * SEL_BLOCK
    m = ((c < j + SEL_BLOCK) & (c + CMP_BLOCK > j) & (np.arange(n_c)[:, None] < n_c - 1)
         & (np.arange(n_sel_pad)[None, :] < n_sel)).astype(np.float64)
    out = np.zeros((n_c + 2 * CPAD, n_sel_pad))
    out[CPAD:CPAD + n_c] = m
    return jnp.asarray(out, BF16)


def _bias_kernel(tbl_ref, bk_ref, o_ref):
    h = pl.program_id(0)
    bk = bk_ref[...]
    base = tbl_ref[REL_BUCKETS - 1, h]
    acc = jnp.zeros(bk.shape, F32)
    for b in range(REL_BUCKETS - 1):
        acc = jnp.where(bk == b, tbl_ref[b, h] - base, acc)
    o_ref[0] = acc


def _bias_tables(rel_bias, tf):
    bk = jnp.asarray(_bucket_tables(tf))
    rows = bk.shape[0]
    return pl.pallas_call(
        _bias_kernel,
        grid=(NSA_H,),
        in_specs=[pl.BlockSpec(memory_space=pltpu.SMEM),
                  pl.BlockSpec((rows, tf), lambda h: (0, 0))],
        out_specs=pl.BlockSpec((1, rows, tf), lambda h: (h, 0, 0)),
        out_shape=jax.ShapeDtypeStruct((NSA_H, rows, tf), F32),
        compiler_params=_cp(("arbitrary",)),
        name="bias_tables",
    )(rel_bias, bk)


def _rope(y, cos, sin):
    lane = _lane(y.shape)
    rot = jnp.where(lane < MLA_NOPE + MLA_ROPE // 2, pltpu.roll(y, LANES - MLA_ROPE // 2, 1),
                    pltpu.roll(y, MLA_ROPE // 2, 1))
    return y * cos + rot * sin


def _four_variants(y):
    lo = _lane(y.shape) < HD
    r = pltpu.roll(y, HD, 1)
    z = jnp.zeros_like(y)
    return [jnp.where(lo, y, z), jnp.where(lo, z, r), jnp.where(lo, r, z), jnp.where(lo, z, y)]


def _with_ones(v, parity):
    return jnp.where(_lane(v.shape) == (HD if parity == 0 else 0), 1.0, v)


def _prep_kernel(x_ref, cos_ref, sin_ref, gv_ref, win_ref, wuq_ref, wuk_ref, wuv_ref, s2_ref,
                 qa_ref, rowsa_ref, ka_ref, va_ref, qb_ref, cmp_ref, slc_ref, wn_ref, gb_ref, qm_ref,
                 gates_ref, ks4_ref, vs4_ref, kw4_ref, vw4_ref):
    x = x_ref[...]
    cos = cos_ref[...]
    sin = sin_ref[...]
    s64 = s2_ref[0]
    sqa = s2_ref[1]

    def gv(i, w=LANES):
        return gv_ref[i:i + 1, 0:w]

    xn = (x * lax.rsqrt(jnp.mean(x * x, axis=-1, keepdims=True) + EPS) * gv(0, D_MODEL)).astype(BF16)

    def proj(a, b):
        return _dot(xn, win_ref[:, a:b])

    ql = proj(C_QL, C_QL + MLA_QL)
    qln = (ql * lax.rsqrt(jnp.mean(ql * ql, axis=-1, keepdims=True) + EPS) * gv(1, MLA_QL)).astype(BF16)
    q = _dot(qln, wuq_ref[...])
    for h in range(MLA_H):
        y = q[:, h * LANES:(h + 1) * LANES]
        yn = y * lax.rsqrt(_seg_ms(y, sqa) + EPS) * gv(3)
        qa_ref[:, h * LANES:(h + 1) * LANES] = _rope(yn, cos, sin).astype(BF16)
    kvl = proj(C_KVL, C_KVL + MLA_KVL)
    cn = kvl * lax.rsqrt(jnp.mean(kvl * kvl, axis=-1, keepdims=True) + EPS) * gv(2)
    krs = proj(C_KR, C_KR + LANES)
    krr = _rope(krs * lax.rsqrt(_seg_ms(krs, sqa) + EPS) * gv(4), cos, sin)
    rowsa_ref[:, 0:MLA_KVL] = cn
    rowsa_ref[:, MLA_KVL:MLA_KVL + MLA_ROPE] = krr[:, MLA_NOPE:MLA_NOPE + MLA_ROPE]
    cb = cn.astype(BF16)
    kraw = _dot(cb, wuk_ref[...])
    for h in range(MLA_H):
        y = kraw[:, h * LANES:(h + 1) * LANES]
        kn = y * lax.rsqrt(_seg_ms(y, s64) + EPS) * gv(5)
        ka_ref[:, h * LANES:(h + 1) * LANES] = (kn + krr).astype(BF16)
    va = _dot(cb, wuv_ref[...])
    for h in range(MLA_H):
        va_ref[:, h * LANES:(h + 1) * LANES] = _with_ones(va[:, h * LANES:(h + 1) * LANES], h % 2).astype(BF16)
    nq = proj(C_NQ, C_NQ + NSA_H * HD)
    for v in range(NSA_H * HD // LANES):
        y = nq[:, v * LANES:(v + 1) * LANES]
        qb_ref[:, v * LANES:(v + 1) * LANES] = (y * lax.rsqrt(_seg_ms(y, s64) + EPS) * gv(6)).astype(BF16)
    nkv = proj(C_NKV, C_NKV + 6 * LANES)
    cmp_ref[...] = nkv[:, 0:2 * LANES]
    for base, gi, rows_ref, k4_ref, v4_ref in ((2 * LANES, 7, slc_ref, ks4_ref, vs4_ref),
                                               (4 * LANES, 8, wn_ref, kw4_ref, vw4_ref)):
        k = nkv[:, base:base + LANES]
        kn = k * lax.rsqrt(_seg_ms(k, s64) + EPS) * gv(gi)
        vv = nkv[:, base + LANES:base + 2 * LANES]
        rows_ref[:, 0:LANES] = kn
        rows_ref[:, LANES:2 * LANES] = vv
        for i, (a, b) in enumerate(zip(_four_variants(kn), _four_variants(vv))):
            k4_ref[:, i * LANES:(i + 1) * LANES] = a.astype(BF16)
            v4_ref[:, i * LANES:(i + 1) * LANES] = _with_ones(b, i % 2).astype(BF16)
    ng = proj(C_NG, C_NG + LANES)
    gb_ref[...] = 1.0 / (1.0 + jnp.exp(-ng))
    mq = proj(C_MQ, C_MQ + MEM_H * HD)
    for v in range(MEM_H * HD // LANES):
        y = mq[:, v * LANES:(v + 1) * LANES]
        qm_ref[:, v * LANES:(v + 1) * LANES] = (y * lax.rsqrt(_seg_ms(y, s64) + EPS) * gv(9)).astype(BF16)
    gt = proj(C_GATE, W_IN_P)
    gates_ref[...] = gt / (1.0 + jnp.exp(-gt))


_PREP_OUT = (("qa", 768, BF16), ("rows_a", 160, F32), ("ka", 768, BF16), ("va", 768, BF16), ("qb", 384, BF16),
             ("rows_cmp", 256, F32), ("rows_slc", 256, F32), ("rows_win", 256, F32), ("gb", 128, F32),
             ("qm", 256, BF16), ("gates", 1024, F32), ("ks4", 512, BF16), ("vs4", 512, BF16),
             ("kw4", 512, BF16), ("vw4", 512, BF16))


def _prep(x2, cos, sin, lw, tm):
    R = x2.shape[0]
    n_tab = cos.shape[0] // tm
    full = lambda a: pl.BlockSpec(a.shape, lambda i: (0,) * a.ndim)
    outs = pl.pallas_call(
        _prep_kernel,
        grid=(R // tm,),
        in_specs=[pl.BlockSpec((tm, D_MODEL), lambda i: (i, 0)),
                  pl.BlockSpec((tm, LANES), lambda i: (i % n_tab, 0)),
                  pl.BlockSpec((tm, LANES), lambda i: (i % n_tab, 0)),
                  full(lw["gv"]), full(lw["w_in"]), full(lw["w_uq"]), full(lw["w_uk_p"]), full(lw["w_uv_p"]),
                  full(lw["s2"])],
        out_specs=[pl.BlockSpec((tm, w), lambda i: (i, 0)) for _, w, _ in _PREP_OUT],
        out_shape=[jax.ShapeDtypeStruct((R, w), dt) for _, w, dt in _PREP_OUT],
        compiler_params=_cp(("arbitrary",)),
        name="prep",
    )(x2, cos, sin, lw["gv"], lw["w_in"], lw["w_uq"], lw["w_uk_p"], lw["w_uv_p"], lw["s2"])
    return {name: o for (name, _, _), o in zip(_PREP_OUT, outs)}


def _flash_kernel(*refs, mode, t):
    if mode == "mla":
        q_ref, k_ref, v_ref, o_ref, m_scr, acc_scr = refs
        sel_ref = bd_ref = None
    elif mode == "slc":
        q_ref, sel_ref, k_ref, v_ref, bd_ref, o_ref, m_scr, acc_scr = refs
    else:
        q_ref, k_ref, v_ref, bd_ref, o_ref, m_scr, acc_scr = refs
        sel_ref = None
    qi = pl.program_id(1)
    m_scr[...] = jnp.full(m_scr.shape, M_INIT, F32)
    acc_scr[...] = jnp.zeros(acc_scr.shape, F32)

    def tile(j, dist):
        rows = pl.ds(pl.multiple_of(j * t, t), t)
        if mode == "slc":
            blk = (j * t + _row((t, LANES))) // SEL_BLOCK
            onehot = (_lane((t, LANES)) == blk).astype(BF16)
        if dist is not None:
            d = dist * t + _row((t, t)) - _lane((t, t))
        for h in range(NSA_H):
            pair = (h // 2) * LANES
            if mode == "mla":
                qh = q_ref[0, :, h * LANES:(h + 1) * LANES]
                kh = k_ref[0, rows, h * LANES:(h + 1) * LANES]
                vh = v_ref[0, rows, h * LANES:(h + 1) * LANES]
            else:
                gp = ((h // NSA_HPG) * 2 + h % 2) * LANES
                qh = q_ref[0, :, pair:pair + LANES]
                kh = k_ref[0, rows, gp:gp + LANES]
                vh = v_ref[0, rows, gp:gp + LANES]
                if mode == "slc":
                    g = h // NSA_HPG
                    qh = jnp.concatenate([qh, sel_ref[0, :, g * LANES:(g + 1) * LANES]], axis=1)
                    kh = jnp.concatenate([kh, onehot], axis=1)
            s = _dot_nt(qh, kh)
            if dist is not None:
                if mode != "mla" and dist <= 1:
                    s = s + bd_ref[h, dist]
                if dist == 0:
                    s = jnp.where(d >= 0, s, NEG)
                elif mode == "win" and (dist + 1) * t > WINDOW:
                    s = jnp.where(d < WINDOW, s, NEG)
            m_prev = m_scr[h]
            m_new = jnp.maximum(m_prev, jnp.max(s, axis=1, keepdims=True))
            alpha = jnp.exp(m_prev - m_new)
            p = jnp.exp(s - m_new[:, 0:1])
            acc_scr[h] = alpha * acc_scr[h] + _dot(p.astype(BF16), vh)
            m_scr[h] = m_new

    def far(j, c):
        tile(j, None)
        return c

    if mode == "mla":
        lax.fori_loop(0, qi, far, 0)
    elif mode == "slc":
        lax.fori_loop(0, jnp.maximum(qi - 1, 0), far, 0)
    near = 1 if mode == "slc" else (-(-WINDOW // t) if mode == "win" else 0)
    for dist in range(near, 0, -1):
        @pl.when(qi >= dist)
        def _(dist=dist):
            tile(qi - dist, dist)
    tile(qi, 0)

    lo = _lane((t, LANES)) < HD
    for pp in range(NSA_H // 2):
        ae = acc_scr[2 * pp]
        ao = acc_scr[2 * pp + 1]
        ra = ae / jnp.maximum(ae[:, HD:HD + 1], 1e-30)
        rb = ao / jnp.maximum(ao[:, 0:1], 1e-30)
        o_ref[0, :, pp * LANES:(pp + 1) * LANES] = jnp.where(lo, ra, rb)


def _flash(mode, q, k, v, sel=None, bd=None):
    N, T, wq = q.shape
    t = bd.shape[-1] if bd is not None else (TF if T % TF == 0 else TQ)
    nq = T // t
    TQ_ = t
    whole = lambda a: pl.BlockSpec((1,) + a.shape[1:], lambda n, i: (n, 0, 0), pipeline_mode=pl.Buffered(1))
    args = [q]
    specs = [pl.BlockSpec((1, TQ_, wq), lambda n, i: (n, i, 0))]
    if mode == "slc":
        args.append(sel)
        specs.append(pl.BlockSpec((1, TQ_, sel.shape[2]), lambda n, i: (n, i, 0)))
    args += [k, v]
    specs += [whole(k), whole(v)]
    if mode != "mla":
        args.append(bd)
        specs.append(pl.BlockSpec(bd.shape, lambda n, i: (0, 0, 0, 0), pipeline_mode=pl.Buffered(1)))
    return pl.pallas_call(
        functools.partial(_flash_kernel, mode=mode, t=t),
        grid=(N, nq),
        in_specs=specs,
        out_specs=pl.BlockSpec((1, t, NSA_H * HD), lambda n, i: (n, i, 0)),
        out_shape=jax.ShapeDtypeStruct((N, T, NSA_H * HD), F32),
        scratch_shapes=[pltpu.VMEM((NSA_H, t, LANES), F32)] * 2,
        compiler_params=_cp(("arbitrary", "arbitrary")),
        name="flash_" + mode,
    )(*args)


def _gather_pages(cache_ref, pt_ref, seq, buf, sem, slot, npg, start, halves=False):
    for j in range(npg):
        rows = pl.ds(j * PAGE, PAGE)
        if halves:
            cps = [pltpu.make_async_copy(cache_ref.at[pt_ref[seq, j], :, pl.ds(hf * LANES, LANES)],
                                         buf.at[slot, hf, rows], sem.at[slot]) for hf in range(2)]
        else:
            cps = [pltpu.make_async_copy(cache_ref.at[pt_ref[seq, j]], buf.at[slot, rows], sem.at[slot])]
        for cp in cps:
            if start:
                cp.start()
            else:
                cp.wait()


def _gather_pages_t(cache_ref, pt_ref, seq, buf, sem, slot, npg, start):
    for j in range(npg):
        cp = pltpu.make_async_copy(cache_ref.at[pt_ref[seq, j]], buf.at[slot, j], sem.at[slot])
        if start:
            cp.start()
        else:
            cp.wait()


def _compress_kernel(pt_ref, cache_ref, w1_ref, pe_ref, b1_ref, w2_ref, gk_ref, kc4_ref, vc4_ref, *scratch,
                     npg, transposed):
    s = pl.program_id(0)
    ns = pl.num_programs(0)
    slot = s % 2
    L = npg * PAGE
    n_c = L // CMP_STRIDE
    if transposed:
        buft, rb, sem = scratch
        gather = functools.partial(_gather_pages_t, cache_ref, pt_ref, buf=buft, sem=sem, npg=npg)
    else:
        buf, sem = scratch
        gather = functools.partial(_gather_pages, cache_ref, pt_ref, buf=buf, sem=sem, npg=npg, halves=True)

    @pl.when(s == 0)
    def _():
        gather(seq=0, slot=0, start=True)

    @pl.when(s + 1 < ns)
    def _():
        gather(seq=s + 1, slot=1 - slot, start=True)

    gather(seq=s, slot=slot, start=False)
    if transposed:
        def to_rows(j, c):
            for hf in range(2):
                rb[hf, pl.ds(pl.multiple_of(j * PAGE, PAGE), PAGE), :] = buft[slot, j, hf * LANES:(hf + 1) * LANES, :].T
            return c
        lax.fori_loop(0, npg, to_rows, 0)
        rows_of = lambda hf, ds: rb[hf, ds, :]
        pad_ref = lambda hf: rb.at[hf]
    else:
        rows_of = lambda hf, ds: buf[slot, hf, ds, :]
        pad_ref = lambda hf: buf.at[slot, hf]
    for hf in range(2):
        pad_ref(hf)[pl.ds(L, CMP_STRIDE), :] = jnp.zeros((CMP_STRIDE, LANES), F32)

    zeros16 = jnp.zeros((CPAD, 4 * LANES), BF16)
    kc4_ref[0, 0:CPAD, :] = zeros16
    kc4_ref[0, CPAD + n_c:2 * CPAD + n_c, :] = zeros16
    vc4_ref[0, 0:CPAD, :] = zeros16
    vc4_ref[0, CPAD + n_c:2 * CPAD + n_c, :] = zeros16

    rc = min(n_c, 256)
    npos = 4
    for c0 in range(0, n_c, rc):
        acc = [jnp.zeros((rc, CMP_HID), F32) for _ in range(4)]
        for p0 in range(0, CMP_BLOCK, npos):
            xs = [[rows_of(hf, pl.ds(c0 * CMP_STRIDE + p0 + i, rc, stride=CMP_STRIDE)) for i in range(npos)]
                  for hf in range(2)]
            for sg in range(4):
                lhs = jnp.concatenate([x[:, (sg % 2) * HD:(sg % 2 + 1) * HD] for x in xs[sg // 2]],
                                      axis=1).astype(BF16)
                acc[sg] = acc[sg] + _dot(lhs, w1_ref[sg // 2, p0 * HD:(p0 + npos) * HD, :])
        outs = []
        for sg in range(4):
            sidx = sg // 2
            b = _dot(pe_ref[sidx], w1_ref[sidx])[0:1, :] + b1_ref[sidx][0:1, :]
            hcur = acc[sg] + b
            hcur = 0.5 * hcur * (1.0 + jnp.tanh(0.7978845608028654 * (hcur + 0.044715 * hcur * hcur * hcur)))
            outs.append(_dot(hcur.astype(BF16), w2_ref[sidx]))
        z = jnp.zeros((rc, HD), F32)
        kn = [o * lax.rsqrt(jnp.mean(o * o, axis=-1, keepdims=True) + EPS) * gk_ref[...] for o in outs[0:2]]
        rows = slice(CPAD + c0, CPAD + c0 + rc)
        for i, (g, p) in enumerate(((0, 0), (0, 1), (1, 0), (1, 1))):
            kk = jnp.concatenate([kn[g], z] if p == 0 else [z, kn[g]], axis=1)
            vv = jnp.concatenate([outs[2 + g], z] if p == 0 else [z, outs[2 + g]], axis=1)
            kc4_ref[0, rows, i * LANES:(i + 1) * LANES] = kk.astype(BF16)
            vc4_ref[0, rows, i * LANES:(i + 1) * LANES] = vv.astype(BF16)


def _compress(pt, cache, lw, npg, transposed):
    n_seq = pt.shape[0]
    L = npg * PAGE
    n_c = L // CMP_STRIDE
    rows = n_c + 2 * CPAD
    full = lambda a: pl.BlockSpec(a.shape, lambda s, pt_: (0,) * a.ndim)
    ws = [lw["cmp_w1"], lw["cmp_pe"], lw["cmp_b1"], lw["cmp_w2"], lw["cmp_gk"]]
    if transposed:
        scratch = [pltpu.VMEM((2, npg, 2 * LANES, PAGE), F32), pltpu.VMEM((2, L + CMP_STRIDE, LANES), F32)]
    else:
        scratch = [pltpu.VMEM((2, 2, L + CMP_STRIDE, LANES), F32)]
    return pl.pallas_call(
        functools.partial(_compress_kernel, npg=npg, transposed=transposed),
        grid_spec=pltpu.PrefetchScalarGridSpec(
            num_scalar_prefetch=1,
            grid=(n_seq,),
            in_specs=[pl.BlockSpec(memory_space=pl.ANY)] + [full(a) for a in ws],
            out_specs=[pl.BlockSpec((1, rows, 4 * LANES), lambda s, pt_: (s, 0, 0))] * 2,
            scratch_shapes=scratch + [pltpu.SemaphoreType.DMA((2,))]),
        out_shape=[jax.ShapeDtypeStruct((n_seq, rows, 4 * LANES), BF16)] * 2,
        compiler_params=_cp(("arbitrary",)),
        name="compress",
    )(pt, cache, *ws)


def _cmp_topk_kernel(q_ref, kc4_ref, vc4_ref, mm_ref, dcn_ref, oc_ref, sel_ref, idx_ref, *, tq, qpos0, n_c, n_sel):
    i = pl.program_id(1)
    q0 = qpos0 + i * tq
    c0p = pl.multiple_of((q0 // TQ) * CPAD, CPAD)
    nsp = mm_ref.shape[1]
    qpos = q0 + _row((tq, 1))
    wn = 2 * CPAD
    colf = _lane((tq, n_c))
    valid_far = colf < (c0p - CPAD)
    c_near = c0p - CPAD + _lane((tq, wn))
    valid_near = (c_near >= 0) & (c_near * CMP_STRIDE + (CMP_BLOCK - 1) <= qpos)
    far_rows = slice(CPAD, CPAD + n_c)
    near_rows = pl.ds(c0p, wn)
    idx_acc = jnp.zeros((tq, LANES), jnp.int32)
    lane_i = _lane((tq, LANES))
    res = [None] * NSA_H
    for g in range(NSA_G):
        ps_far = jnp.zeros((tq, n_c), F32)
        ps_near = jnp.zeros((tq, wn), F32)
        for hp in range(NSA_HPG):
            h = g * NSA_HPG + hp
            gp = (g * 2 + h % 2) * LANES
            qh = q_ref[0, :, (h // 2) * LANES:(h // 2 + 1) * LANES]
            s_far = jnp.where(valid_far, _dot_nt(qh, kc4_ref[0, far_rows, gp:gp + LANES]), NEG)
            s_near = _dot_nt(qh, kc4_ref[0, near_rows, gp:gp + LANES]) + dcn_ref[h][:, 0:wn]
            s_near = jnp.where(valid_near, s_near, NEG)
            m = jnp.maximum(jnp.maximum(jnp.max(s_far, axis=1, keepdims=True), jnp.max(s_near, axis=1, keepdims=True)),
                            M_INIT)
            pf = jnp.exp(s_far - m)
            pn = jnp.exp(s_near - m)
            inv = 1.0 / jnp.maximum(jnp.sum(pf, axis=1, keepdims=True) + jnp.sum(pn, axis=1, keepdims=True), 1e-30)
            pf = pf * inv
            pn = pn * inv
            res[h] = (_dot(pf.astype(BF16), vc4_ref[0, far_rows, gp:gp + LANES])
                      + _dot(pn.astype(BF16), vc4_ref[0, near_rows, gp:gp + LANES]))
            ps_far = ps_far + pf
            ps_near = ps_near + pn
        mf = mm_ref[far_rows, :]
        mn = mm_ref[near_rows, :]
        imp = (_dot(_split_bf16(ps_far), jnp.concatenate([mf, mf], axis=0))
               + _dot(_split_bf16(ps_near), jnp.concatenate([mn, mn], axis=0)))
        jl = _lane((tq, nsp))
        cur = qpos // SEL_BLOCK
        forced = (jl == 0) | (jl == cur) | (jl == cur - 1)
        valid = jl * SEL_BLOCK <= qpos
        imp = jnp.where(valid, imp + jnp.where(forced, FORCE_BONUS, 0.0), NEG)
        imp = jnp.where(jl < n_sel, imp, PAD_IMP)
        jf = jl.astype(F32)
        sel = jnp.zeros((tq, nsp), jnp.bool_)
        for kk in range(TOP_N):
            mx = jnp.max(imp, axis=1, keepdims=True)
            am = jnp.min(jnp.where(imp == mx, jf, 1e9), axis=1, keepdims=True)
            hit = jf == am
            sel = sel | hit
            imp = jnp.where(hit, -jnp.inf, imp)
            idx_acc = jnp.where(lane_i == g * TOP_N + kk, am.astype(jnp.int32), idx_acc)
        sel_ref[0, :, g * nsp:(g + 1) * nsp] = jnp.where(sel, 0.0, NEG).astype(BF16)
    for pp in range(NSA_H // 2):
        oc_ref[0, :, pp * LANES:(pp + 1) * LANES] = res[2 * pp] + res[2 * pp + 1]
    idx_ref[0] = idx_acc


def _cmp_topk(qb, kc4, vc4, dcn, tq, qpos0, n_sel):
    n_seq, T, _ = qb.shape
    rows = kc4.shape[1]
    n_c = rows - 2 * CPAD
    nsp = -(-n_sel // LANES) * LANES
    mm = _cmp_to_sel(n_c, n_sel, nsp)
    whole = lambda a: pl.BlockSpec((1,) + a.shape[1:], lambda n, i: (n, 0, 0))
    return pl.pallas_call(
        functools.partial(_cmp_topk_kernel, tq=tq, qpos0=qpos0, n_c=n_c, n_sel=n_sel),
        grid=(n_seq, T // tq),
        in_specs=[pl.BlockSpec((1, tq, NSA_H * HD), lambda n, i: (n, i, 0)), whole(kc4), whole(vc4),
                  pl.BlockSpec(mm.shape, lambda n, i: (0, 0)),
                  pl.BlockSpec(dcn.shape, lambda n, i: (0, 0, 0))],
        out_specs=[pl.BlockSpec((1, tq, NSA_H * HD), lambda n, i: (n, i, 0)),
                   pl.BlockSpec((1, tq, NSA_G * nsp), lambda n, i: (n, i, 0)),
                   pl.BlockSpec((1, tq, LANES), lambda n, i: (n, i, 0))],
        out_shape=[jax.ShapeDtypeStruct((n_seq, T, NSA_H * HD), F32),
                   jax.ShapeDtypeStruct((n_seq, T, NSA_G * nsp), BF16),
                   jax.ShapeDtypeStruct((n_seq, T, LANES), jnp.int32)],
        compiler_params=_cp(("arbitrary", "arbitrary")),
        name="cmp_topk",
    )(qb, kc4, vc4, mm, dcn)


def _mem_kv_kernel(mem_ref, g_ref, w_ref, s2_ref, gk_ref, o_ref):
    x = mem_ref[0]
    xn = (x * lax.rsqrt(jnp.mean(x * x, axis=-1, keepdims=True) + EPS) * g_ref[...]).astype(BF16)
    kv = _dot(xn, w_ref[...])
    nk = MEM_H * HD
    for v in range(nk // LANES):
        y = kv[:, v * LANES:(v + 1) * LANES]
        o_ref[0, :, v * LANES:(v + 1) * LANES] = y * lax.rsqrt(_seg_ms(y, s2_ref[0]) + EPS) * gk_ref[...]
    o_ref[0, :, nk:2 * nk] = kv[:, nk:2 * nk]


def _mem_kv(mem, lw):
    N, M, _ = mem.shape
    full = lambda a: pl.BlockSpec(a.shape, lambda n: (0,) * a.ndim)
    ws = [lw["mem_norm_g"], lw["w_mem_kv"], lw["s2"], lw["mem_gk"]]
    return pl.pallas_call(
        _mem_kv_kernel,
        grid=(N,),
        in_specs=[pl.BlockSpec((1, M, D_MODEL), lambda n: (n, 0, 0))] + [full(a) for a in ws],
        out_specs=pl.BlockSpec((1, M, 2 * MEM_H * HD), lambda n: (n, 0, 0)),
        out_shape=jax.ShapeDtypeStruct((N, M, 2 * MEM_H * HD), F32),
        compiler_params=_cp(("arbitrary",)),
        name="mem_kv",
    )(mem, *ws)


def _mem_attn_kernel(q_ref, kv_ref, o_ref):
    nk = MEM_H * HD
    tm = q_ref.shape[1]
    lo = _lane((kv_ref.shape[1], LANES)) < HD
    lo_o = _lane((tm, LANES)) < HD
    for pp in range(MEM_H // 2):
        qp = q_ref[0, :, pp * LANES:(pp + 1) * LANES]
        kp = kv_ref[0, :, pp * LANES:(pp + 1) * LANES]
        vp = kv_ref[0, :, nk + pp * LANES:nk + (pp + 1) * LANES].astype(BF16)
        r = []
        for par in range(2):
            kz = jnp.where(lo if par == 0 else ~lo, kp, 0.0).astype(BF16)
            s = _dot_nt(qp, kz)
            p = jnp.exp(s - jnp.max(s, axis=1, keepdims=True))
            r.append(_dot(p.astype(BF16), vp) / jnp.sum(p, axis=1, keepdims=True))
        o_ref[0, :, pp * LANES:(pp + 1) * LANES] = jnp.where(lo_o, r[0], r[1])


def _mem_attn(qm, mkv, tm):
    N, T, w = qm.shape
    return pl.pallas_call(
        _mem_attn_kernel,
        grid=(N, T // tm),
        in_specs=[pl.BlockSpec((1, tm, w), lambda n, i: (n, i, 0)),
                  pl.BlockSpec((1,) + mkv.shape[1:], lambda n, i: (n, 0, 0))],
        out_specs=pl.BlockSpec((1, tm, w), lambda n, i: (n, i, 0)),
        out_shape=jax.ShapeDtypeStruct((N, T, w), F32),
        compiler_params=_cp(("arbitrary", "arbitrary")),
        name="mem_attn",
    )(qm, mkv)


def _mla_decode_kernel(pt_ref, cache_ref, q6_ref, qr_ref, own_ref, wuk_ref, wukt_ref, wuv_ref, gk_ref, o_ref,
                       buf, sem, *, npg, pc):
    s = pl.program_id(0)
    ns = pl.num_programs(0)
    slot = s % 2
    w = MLA_H * HD

    @pl.when(s == 0)
    def _():
        _gather_pages_t(cache_ref, pt_ref, 0, buf, sem, 0, npg, True)

    @pl.when(s + 1 < ns)
    def _():
        _gather_pages_t(cache_ref, pt_ref, s + 1, buf, sem, 1 - slot, npg, True)

    _gather_pages_t(cache_ref, pt_ref, s, buf, sem, slot, npg, False)
    q6g = q6_ref[0].astype(F32) * gk_ref[...]
    qabs = _dot_nt(q6g.astype(BF16), wuk_ref[...]).astype(BF16)
    qr = qr_ref[0][:, 0:MLA_ROPE]
    rowi = _row((8, pc * PAGE))

    def body(i, carry):
        m_prev, l_prev, acc = carry
        ct = jnp.concatenate([buf[slot, i * pc + j, 0:MLA_KVL, :] for j in range(pc)], axis=1).astype(BF16)
        krt = jnp.concatenate([buf[slot, i * pc + j, MLA_KVL:MLA_KVL + MLA_ROPE, :] for j in range(pc)],
                              axis=1).astype(BF16)
        kraw = _dot(wukt_ref[...], ct)
        sq = kraw * kraw
        ms = jnp.zeros((8, pc * PAGE), F32)
        for h in range(MLA_H):
            ms = jnp.where(rowi == h, jnp.sum(sq[h * HD:(h + 1) * HD], axis=0, keepdims=True), ms)
        sc = _dot(qabs, ct) * lax.rsqrt(ms * (1.0 / HD) + EPS) + _dot(qr, krt)
        m_new = jnp.maximum(m_prev, jnp.max(sc, axis=1, keepdims=True))
        alpha = jnp.exp(m_prev - m_new)
        p = jnp.exp(sc - m_new)
        return (m_new, alpha * l_prev + jnp.sum(p, axis=1, keepdims=True),
                alpha * acc + _dot_nt(p.astype(BF16), ct))

    carry = (jnp.full((8, 1), M_INIT, F32), jnp.zeros((8, 1), F32), jnp.zeros((8, MLA_KVL), F32))
    m_prev, l_prev, acc = lax.fori_loop(0, npg // pc, body, carry)
    own = own_ref[0]
    c_own = own[0:1, 0:MLA_KVL]
    kraw_o = _dot(own[:, 0:MLA_KVL].astype(BF16), wuk_ref[...])[0:1, :]
    own_head = (_lane((8, w)) // HD) == _row((8, w))
    ms_o = jnp.sum(jnp.where(own_head, kraw_o * kraw_o, 0.0), axis=1, keepdims=True)
    s_own = (jnp.sum(q6g * kraw_o, axis=1, keepdims=True) * lax.rsqrt(ms_o * (1.0 / HD) + EPS)
             + jnp.sum(qr.astype(F32) * own[0:1, MLA_KVL:MLA_KVL + MLA_ROPE], axis=1, keepdims=True))
    m_new = jnp.maximum(m_prev, s_own)
    alpha = jnp.exp(m_prev - m_new)
    p_own = jnp.exp(s_own - m_new)
    l_fin = alpha * l_prev + p_own
    acc = alpha * acc + p_own * c_own
    o_lat = (acc / jnp.maximum(l_fin, 1e-30)).astype(BF16)
    res = _dot(o_lat, wuv_ref[...])
    o_ref[0] = jnp.broadcast_to(jnp.sum(jnp.where(own_head, res, 0.0), axis=0, keepdims=True), (8, w))


def _mla_decode(pt, cache, q6, qr, own, lw, npg):
    DB = pt.shape[0]
    pc = 4 if npg % 4 == 0 else 1
    full = lambda a: pl.BlockSpec(a.shape, lambda s, pt_: (0,) * a.ndim)
    per = lambda a: pl.BlockSpec((1,) + a.shape[1:], lambda s, pt_: (s, 0, 0))
    ws = [lw["w_uk"], lw["w_uk_t"], lw["w_uv"], lw["mla_gk"]]
    w = MLA_H * HD
    return pl.pallas_call(
        functools.partial(_mla_decode_kernel, npg=npg, pc=pc),
        grid_spec=pltpu.PrefetchScalarGridSpec(
            num_scalar_prefetch=1,
            grid=(DB,),
            in_specs=[pl.BlockSpec(memory_space=pl.ANY), per(q6), per(qr), per(own)] + [full(a) for a in ws],
            out_specs=pl.BlockSpec((1, 8, w), lambda s, pt_: (s, 0, 0)),
            scratch_shapes=[pltpu.VMEM((2, npg, MLA_KVL + MLA_ROPE, PAGE), F32), pltpu.SemaphoreType.DMA((2,))]),
        out_shape=jax.ShapeDtypeStruct((DB, 8, w), F32),
        compiler_params=_cp(("arbitrary",)),
        name="mla_decode",
    )(pt, cache, q6, qr, own, *ws)


def _decode_core(q8, keys, bias, valid, own_row, own_bias, own_flag):
    kb = keys.astype(BF16)
    s = _dot_nt(q8, kb) + bias
    if valid is not None:
        s = jnp.where(valid, s, NEG)
    s_own = jnp.sum(q8.astype(F32) * own_row, axis=1, keepdims=True) + own_bias
    s_own = jnp.where(own_flag, s_own, NEG)
    m = jnp.maximum(jnp.maximum(jnp.max(s, axis=1, keepdims=True), s_own), M_INIT)
    p = jnp.exp(s - m)
    p_own = jnp.exp(s_own - m)
    l = jnp.sum(p, axis=1, keepdims=True) + p_own
    return (_dot(p.astype(BF16), kb) + p_own * own_row) / jnp.maximum(l, 1e-30)


def _place_heads(res):
    v0 = res[0][:, 2 * HD:4 * HD]
    v1 = res[1][:, 2 * HD:4 * HD]
    v0r = pltpu.roll(v0, HD, 1)
    v1r = pltpu.roll(v1, HD, 1)
    lo = _lane((1, LANES)) < HD
    row = jnp.concatenate([jnp.where(lo, v0[0:1], v0r[1:2]), jnp.where(lo, v0[2:3], v1[0:1]),
                           jnp.where(lo, v1r[1:2], v1[2:3])], axis=1)
    return jnp.broadcast_to(row, (8, NSA_H * HD))


def _slc_decode_kernel(idx_ref, pt_ref, cache_ref, q8_ref, own_ref, bt_ref, d0_ref, o_ref, buf, sem, *, n_past_blk):
    s = pl.program_id(0)
    ns = pl.num_programs(0)
    slot = s % 2
    nblk = NSA_G * TOP_N
    bpp = PAGE // SEL_BLOCK

    def copies(seq, sl, start):
        for j in range(nblk):
            b = jnp.minimum(idx_ref[seq, j], n_past_blk - 1)
            src = pt_ref[seq, b // bpp] * bpp + b % bpp
            cp = pltpu.make_async_copy(cache_ref.at[src], buf.at[sl, pl.ds(j * SEL_BLOCK, SEL_BLOCK)], sem.at[sl])
            if start:
                cp.start()
            else:
                cp.wait()

    @pl.when(s == 0)
    def _():
        copies(0, 0, True)

    @pl.when(s + 1 < ns)
    def _():
        copies(s + 1, 1 - slot, True)

    copies(s, slot, False)
    own = own_ref[0][0:1, :]
    lo = _lane((8, LANES)) < HD
    nk = TOP_N * SEL_BLOCK
    res = []
    for g in range(NSA_G):
        pieces, vpieces = [], []
        own_flag = idx_ref[s, g * TOP_N] == n_past_blk
        for jj in range(TOP_N // 2):
            ia = idx_ref[s, g * TOP_N + 2 * jj]
            ib = idx_ref[s, g * TOP_N + 2 * jj + 1]
            own_flag = own_flag | (ia == n_past_blk) | (ib == n_past_blk)
            z = jnp.zeros((8, LANES), F32)
            bias = (jnp.where(ia == n_past_blk - 1, bt_ref[g, 0], z) + jnp.where(ia == n_past_blk - 2, bt_ref[g, 2], z)
                    + jnp.where(ib == n_past_blk - 1, bt_ref[g, 1], z) + jnp.where(ib == n_past_blk - 2, bt_ref[g, 3], z))
            pieces.append(bias)
            va = jnp.where(ia < n_past_blk, 1, 0)
            vb = jnp.where(ib < n_past_blk, 1, 0)
            vpieces.append(jnp.where(lo, va, vb) > 0)
        keys = buf[slot, g * nk:(g + 1) * nk, :]
        res.append(_decode_core(q8_ref[0, g], keys, jnp.concatenate(pieces, axis=1),
                                jnp.concatenate(vpieces, axis=1), own, d0_ref[g][:, 0:1], own_flag))
    o_ref[0] = _place_heads(res)


def _slc_decode(idx, pt, cache, q8, own, bt, d0, n_past_blk):
    DB = idx.shape[0]
    per3 = lambda a: pl.BlockSpec((1,) + a.shape[1:], lambda s, i_, p_: (s,) + (0,) * (a.ndim - 1))
    full = lambda a: pl.BlockSpec(a.shape, lambda s, i_, p_: (0,) * a.ndim)
    return pl.pallas_call(
        functools.partial(_slc_decode_kernel, n_past_blk=n_past_blk),
        grid_spec=pltpu.PrefetchScalarGridSpec(
            num_scalar_prefetch=2,
            grid=(DB,),
            in_specs=[pl.BlockSpec(memory_space=pl.ANY), per3(q8), per3(own), full(bt), full(d0)],
            out_specs=pl.BlockSpec((1, 8, NSA_H * HD), lambda s, i_, p_: (s, 0, 0)),
            scratch_shapes=[pltpu.VMEM((2, NSA_G * TOP_N * SEL_BLOCK, 2 * LANES), F32), pltpu.SemaphoreType.DMA((2,))]),
        out_shape=jax.ShapeDtypeStruct((DB, 8, NSA_H * HD), F32),
        compiler_params=_cp(("arbitrary",)),
        name="slc_decode",
    )(idx, pt, cache, q8, own, bt, d0)


def _win_decode_kernel(wb_ref, q8_ref, own_ref, bw_ref, d0_ref, o_ref, *, first_valid):
    keys = wb_ref[0]
    n = keys.shape[0]
    own = own_ref[0][0:1, :]
    valid = _lane((8, n)) >= first_valid
    o_ref[0] = _place_heads([_decode_core(q8_ref[0, g], keys, bw_ref[g], valid, own, d0_ref[g][:, 0:1], True)
                             for g in range(NSA_G)])


def _win_decode(wbuf, q8, own, bw, d0, first_valid):
    DB, wb, _ = wbuf.shape
    per = lambda a: pl.BlockSpec((1,) + a.shape[1:], lambda s: (s,) + (0,) * (a.ndim - 1))
    full = lambda a: pl.BlockSpec(a.shape, lambda s: (0,) * a.ndim)
    return pl.pallas_call(
        functools.partial(_win_decode_kernel, first_valid=first_valid),
        grid=(DB,),
        in_specs=[per(wbuf), per(q8), per(own), full(bw), full(d0)],
        out_specs=pl.BlockSpec((1, 8, NSA_H * HD), lambda s: (s, 0, 0)),
        out_shape=jax.ShapeDtypeStruct((DB, 8, NSA_H * HD), F32),
        compiler_params=_cp(("arbitrary",)),
        name="win_decode",
    )(wbuf, q8, own, bw, d0)


def _mem_decode_kernel(kv_ref, q8_ref, o_ref):
    kb = kv_ref[0].astype(BF16)
    s = _dot_nt(q8_ref[0], kb)
    p = jnp.exp(s - jnp.max(s, axis=1, keepdims=True))
    nk = MEM_H * HD
    res = (_dot(p.astype(BF16), kb) / jnp.sum(p, axis=1, keepdims=True))[:, nk:2 * nk]
    own_head = (_lane((8, nk)) // HD) == _row((8, nk))
    o_ref[0] = jnp.broadcast_to(jnp.sum(jnp.where(own_head, res, 0.0), axis=0, keepdims=True), (8, nk))


def _mem_decode(mkv, q8):
    DB, M, w = mkv.shape
    return pl.pallas_call(
        _mem_decode_kernel,
        grid=(DB,),
        in_specs=[pl.BlockSpec((1, M, w), lambda s: (s, 0, 0)), pl.BlockSpec((1, 8, w), lambda s: (s, 0, 0))],
        out_specs=pl.BlockSpec((1, 8, w // 2), lambda s: (s, 0, 0)),
        out_shape=jax.ShapeDtypeStruct((DB, 8, w // 2), F32),
        compiler_params=_cp(("arbitrary",)),
        name="mem_decode",
    )(mkv, q8)


def _decode_group(qg, kt, vt, bias, valid, k_own, v_own, own_bias, own_flag):
    s = _dot(qg.astype(BF16), kt.astype(BF16)) + bias
    s = jnp.where(valid, s, NEG)
    s_own = jnp.sum(qg * k_own, axis=1, keepdims=True) + own_bias
    s_own = jnp.where(own_flag, s_own, NEG)
    m = jnp.maximum(jnp.maximum(jnp.max(s, axis=1, keepdims=True), s_own), M_INIT)
    p = jnp.exp(s - m)
    p_own = jnp.exp(s_own - m)
    l = jnp.sum(p, axis=1, keepdims=True) + p_own
    return (_dot_nt(p.astype(BF16), vt.astype(BF16)) + p_own * v_own) / jnp.maximum(l, 1e-30)


def _head_row(res):
    d0 = jnp.concatenate([res[0], res[0]], axis=1)
    d1 = jnp.concatenate([res[1], res[1]], axis=1)
    lo = _lane((1, LANES)) < HD
    row = jnp.concatenate([jnp.where(lo, d0[0:1], d0[1:2]), jnp.where(lo, d0[2:3], d1[0:1]),
                           jnp.where(lo, d1[1:2], d1[2:3])], axis=1)
    return jnp.broadcast_to(row, (8, NSA_H * HD))


def _slc_decode_t_kernel(idx_ref, pt_ref, cache_ref, qg_ref, own_ref, bt_ref, d0_ref, o_ref, buf, sem, *, n_past_blk):
    s = pl.program_id(0)
    ns = pl.num_programs(0)
    slot = s % 2
    nblk = NSA_G * TOP_N
    bpp = PAGE // SEL_BLOCK

    def copies(seq, sl, start):
        for j in range(nblk):
            b = jnp.minimum(idx_ref[seq, j], n_past_blk - 1)
            cp = pltpu.make_async_copy(cache_ref.at[pt_ref[seq, b // bpp]], buf.at[sl, j], sem.at[sl])
            if start:
                cp.start()
            else:
                cp.wait()

    @pl.when(s == 0)
    def _():
        copies(0, 0, True)

    @pl.when(s + 1 < ns)
    def _():
        copies(s + 1, 1 - slot, True)

    copies(s, slot, False)
    own = own_ref[0][0:1, :]
    lane = _lane((8, PAGE))
    res = []
    for g in range(NSA_G):
        pieces, vpieces = [], []
        own_flag = idx_ref[s, g * TOP_N] == n_past_blk
        for j in range(TOP_N):
            ib = idx_ref[s, g * TOP_N + j]
            own_flag = own_flag | (ib == n_past_blk)
            z = jnp.zeros((8, PAGE), F32)
            pieces.append(jnp.where(ib == n_past_blk - 1, bt_ref[g, 0], z) + jnp.where(ib == n_past_blk - 2, bt_ref[g, 1], z))
            half = jnp.where(ib < n_past_blk, ib % bpp, bpp)
            vpieces.append((lane // SEL_BLOCK) == half)
        kt = jnp.concatenate([buf[slot, g * TOP_N + j, g * HD:(g + 1) * HD, :] for j in range(TOP_N)], axis=1)
        vt = jnp.concatenate([buf[slot, g * TOP_N + j, (NSA_G + g) * HD:(NSA_G + g + 1) * HD, :] for j in range(TOP_N)],
                             axis=1)
        res.append(_decode_group(qg_ref[0, g * 8:(g + 1) * 8, :], kt, vt, jnp.concatenate(pieces, axis=1),
                                 jnp.concatenate(vpieces, axis=1), own[:, g * HD:(g + 1) * HD],
                                 own[:, (NSA_G + g) * HD:(NSA_G + g + 1) * HD], d0_ref[g][:, 0:1], own_flag))
    o_ref[0] = _head_row(res)


def _slc_decode_t(idx, pt, cache, qg, own, bt, d0, n_past_blk):
    DB = idx.shape[0]
    per3 = lambda a: pl.BlockSpec((1,) + a.shape[1:], lambda s, i_, p_: (s,) + (0,) * (a.ndim - 1))
    full = lambda a: pl.BlockSpec(a.shape, lambda s, i_, p_: (0,) * a.ndim)
    return pl.pallas_call(
        functools.partial(_slc_decode_t_kernel, n_past_blk=n_past_blk),
        grid_spec=pltpu.PrefetchScalarGridSpec(
            num_scalar_prefetch=2,
            grid=(DB,),
            in_specs=[pl.BlockSpec(memory_space=pl.ANY), per3(qg), per3(own), full(bt), full(d0)],
            out_specs=pl.BlockSpec((1, 8, NSA_H * HD), lambda s, i_, p_: (s, 0, 0)),
            scratch_shapes=[pltpu.VMEM((2, NSA_G * TOP_N, 2 * LANES, PAGE), F32), pltpu.SemaphoreType.DMA((2,))]),
        out_shape=jax.ShapeDtypeStruct((DB, 8, NSA_H * HD), F32),
        compiler_params=_cp(("arbitrary",)),
        name="slc_decode",
    )(idx, pt, cache, qg, own, bt, d0)


def _win_decode_t_kernel(wt_ref, qg_ref, own_ref, bw_ref, d0_ref, o_ref, *, first_valid):
    n = wt_ref.shape[2]
    own = own_ref[0][0:1, :]
    valid = _lane((8, n)) >= first_valid
    res = [_decode_group(qg_ref[0, g * 8:(g + 1) * 8, :], wt_ref[0, g * HD:(g + 1) * HD, :],
                         wt_ref[0, (NSA_G + g) * HD:(NSA_G + g + 1) * HD, :], bw_ref[g], valid,
                         own[:, g * HD:(g + 1) * HD], own[:, (NSA_G + g) * HD:(NSA_G + g + 1) * HD],
                         d0_ref[g][:, 0:1], True) for g in range(NSA_G)]
    o_ref[0] = _head_row(res)


def _win_decode_t(wt, qg, own, bw, d0, first_valid):
    DB = wt.shape[0]
    per = lambda a: pl.BlockSpec((1,) + a.shape[1:], lambda s: (s,) + (0,) * (a.ndim - 1))
    full = lambda a: pl.BlockSpec(a.shape, lambda s: (0,) * a.ndim)
    return pl.pallas_call(
        functools.partial(_win_decode_t_kernel, first_valid=first_valid),
        grid=(DB,),
        in_specs=[per(wt), per(qg), per(own), full(bw), full(d0)],
        out_specs=pl.BlockSpec((1, 8, NSA_H * HD), lambda s: (s, 0, 0)),
        out_shape=jax.ShapeDtypeStruct((DB, 8, NSA_H * HD), F32),
        compiler_params=_cp(("arbitrary",)),
        name="win_decode",
    )(wt, qg, own, bw, d0)


def _mem_decode_t_kernel(kvt_ref, q8_ref, o_ref):
    nk = MEM_H * HD
    s = _dot(q8_ref[0], kvt_ref[0, 0:nk, :].astype(BF16))
    p = jnp.exp(s - jnp.max(s, axis=1, keepdims=True))
    res = _dot_nt(p.astype(BF16), kvt_ref[0, nk:2 * nk, :].astype(BF16)) / jnp.sum(p, axis=1, keepdims=True)
    own_head = (_lane((8, nk)) // HD) == _row((8, nk))
    o_ref[0] = jnp.broadcast_to(jnp.sum(jnp.where(own_head, res, 0.0), axis=0, keepdims=True), (8, nk))


def _mem_decode_t(kvt, q8):
    DB, w, M = kvt.shape
    return pl.pallas_call(
        _mem_decode_t_kernel,
        grid=(DB,),
        in_specs=[pl.BlockSpec((1, w, M), lambda s: (s, 0, 0)), pl.BlockSpec((1, 8, w // 2), lambda s: (s, 0, 0))],
        out_specs=pl.BlockSpec((1, 8, w // 2), lambda s: (s, 0, 0)),
        out_shape=jax.ShapeDtypeStruct((DB, 8, w // 2), F32),
        compiler_params=_cp(("arbitrary",)),
        name="mem_decode",
    )(kvt, q8)


def _merge_kernel(x_ref, oa_ref, oc_ref, os_ref, ow_ref, gb_ref, om_ref, gates_ref, e2_ref, wout_ref, y_ref):
    w = NSA_H * HD
    ge = _dot(_split_bf16(gb_ref[...]), e2_ref[...])
    ob = oc_ref[...] * ge[:, 0:w] + os_ref[...] * ge[:, w:2 * w] + ow_ref[...] * ge[:, 2 * w:3 * w]
    z = jnp.concatenate([oa_ref[...], ob, om_ref[...]], axis=1) * gates_ref[...]
    y_ref[...] = x_ref[...] + _dot(z.astype(BF16), wout_ref[...])


def _merge(x2, oa, oc, osl, ow, gb, om, gates, lw, tm):
    R = x2.shape[0]
    row = lambda a: pl.BlockSpec((tm, a.shape[1]), lambda i: (i, 0))
    full = lambda a: pl.BlockSpec(a.shape, lambda i: (0,) * a.ndim)
    acts = [x2, oa, oc, osl, ow, gb, om, gates]
    return pl.pallas_call(
        _merge_kernel,
        grid=(R // tm,),
        in_specs=[row(a) for a in acts] + [full(lw["e2"]), full(lw["w_out"])],
        out_specs=pl.BlockSpec((tm, D_MODEL), lambda i: (i, 0)),
        out_shape=jax.ShapeDtypeStruct((R, D_MODEL), F32),
        compiler_params=_cp(("arbitrary",)),
        name="merge",
    )(*acts, lw["e2"], lw["w_out"])


def _layer_weights(l, norm_g, w_in, mla_q_norm, mla_w_uq, mla_kv_norm, mla_w_ukv, mla_nope_g, mla_rope_g,
                   nsa_qk_g, nsa_cmp_pe, nsa_cmp_w1, nsa_cmp_b1, nsa_cmp_w2, mem_norm_g, w_mem_kv, mem_qk_g, w_out):
    w = w_in[l]
    o = np.cumsum((0, 256, 128, 32, 384, 384, 768, 18, 384, 256, 256))
    z = lambda n: jnp.zeros((D_MODEL, n), F32)
    w_in_p = jnp.concatenate([
        w[:, o[0]:o[1]], w[:, o[1]:o[2]],
        z(MLA_NOPE), w[:, o[2]:o[3]], z(LANES - MLA_NOPE - MLA_ROPE),
        w[:, o[4]:o[5]], w[:, o[5]:o[6]],
        w[:, o[6]:o[7]], z(LANES - 18),
        w[:, o[8]:o[9]],
        w[:, o[3]:o[4]], w[:, o[7]:o[8]], w[:, o[9]:o[10]]], axis=1).astype(BF16)
    uq = mla_w_uq[l].reshape(MLA_QL, MLA_H, MLA_QK)
    w_uq = jnp.pad(uq, ((0, 0), (0, 0), (0, LANES - MLA_QK))).reshape(MLA_QL, MLA_H * LANES).astype(BF16)
    ukv = mla_w_ukv[l].reshape(MLA_KVL, MLA_H, 2 * HD)
    w_uk = ukv[:, :, :HD].reshape(MLA_KVL, MLA_H * HD).astype(BF16)
    w_uk_p = jnp.pad(ukv[:, :, :HD], ((0, 0), (0, 0), (0, HD))).reshape(MLA_KVL, MLA_H * LANES).astype(BF16)
    w_uv = ukv[:, :, HD:].reshape(MLA_KVL, MLA_H * HD).astype(BF16)
    uv2 = ukv[:, :, HD:].reshape(MLA_KVL, MLA_H // 2, 2, HD)
    zv = jnp.zeros_like(uv2[:, :, 0])
    w_uv_p = jnp.stack([uv2[:, :, 0], zv, zv, uv2[:, :, 1]], axis=2).reshape(MLA_KVL, MLA_H * LANES).astype(BF16)
    sa = MLA_QK ** -0.5
    sb = HD ** -0.5
    pad = lambda v: jnp.pad(v, (0, D_MODEL - v.shape[0]))
    z32 = jnp.zeros((32,), F32)
    z64 = jnp.zeros((64,), F32)
    qk = nsa_qk_g[l]
    gv = jnp.stack([
        norm_g[l], pad(mla_q_norm[l]), pad(mla_kv_norm[l]),
        pad(jnp.concatenate([mla_nope_g[l, 0] * sa, mla_rope_g[l, 0] * sa, z32])),
        pad(jnp.concatenate([z64, mla_rope_g[l, 1], z32])),
        pad(jnp.concatenate([mla_nope_g[l, 1], z64])),
        pad(jnp.tile(qk[0] * sb, 2)), pad(jnp.tile(qk[2], 2)), pad(jnp.tile(qk[3], 2)),
        pad(jnp.tile(mem_qk_g[l, 0] * sb, 2))] + [jnp.zeros((D_MODEL,), F32)] * 6)
    pe = nsa_cmp_pe[l].reshape(2, 1, CMP_BLOCK * HD)
    return dict(
        gv=gv, w_in=w_in_p, w_uq=w_uq, w_uk=w_uk, w_uk_t=w_uk.T, w_uk_p=w_uk_p, w_uv=w_uv, w_uv_p=w_uv_p, s2=_seg_mats(),
        mla_gk=jnp.tile(mla_nope_g[l, 1], MLA_H)[None, :],
        cmp_w1=nsa_cmp_w1[l].astype(BF16), cmp_pe=jnp.broadcast_to(pe, (2, 8, CMP_BLOCK * HD)).astype(BF16),
        cmp_b1=jnp.broadcast_to(nsa_cmp_b1[l][:, None, :], (2, 8, CMP_HID)),
        cmp_w2=nsa_cmp_w2[l].astype(BF16), cmp_gk=qk[1][None, :],
        mem_norm_g=mem_norm_g[l][None, :], w_mem_kv=w_mem_kv[l].astype(BF16),
        mem_gk=jnp.tile(mem_qk_g[l, 1], 2)[None, :],
        e2=_gate_expand(), w_out=w_out[l].astype(BF16))


def _rope_tables(pos):
    half = MLA_ROPE // 2
    inv = ROPE_THETA ** (-jnp.arange(half, dtype=F32) / half)
    ang = pos.astype(F32)[:, None] * inv[None, :]
    c, s = jnp.cos(ang), jnp.sin(ang)
    n = pos.shape[0]
    cos = jnp.concatenate([jnp.ones((n, MLA_NOPE), F32), c, c, jnp.zeros((n, LANES - MLA_QK), F32)], axis=1)
    sin = jnp.concatenate([jnp.zeros((n, MLA_NOPE), F32), -s, s, jnp.zeros((n, LANES - MLA_QK), F32)], axis=1)
    return cos, sin


def _group_queries(qb):
    DB = qb.shape[0]
    qh = qb.reshape(DB, NSA_G, NSA_HPG, HD)
    out = jnp.zeros((DB, NSA_G, 8, 4, HD), qb.dtype)
    for g in range(NSA_G):
        out = out.at[:, g, :NSA_HPG, g].set(qh[:, g])
    return out.reshape(DB, NSA_G, 8, 4 * HD)


def kernel(x_prompt, x_sample, mem_prompt, cache_mla, cache_nsa_cmp, cache_nsa_slc, cache_nsa_win, cache_mem_kv,
           page_table, norm_g, w_in, mla_q_norm, mla_w_uq, mla_kv_norm, mla_w_ukv, mla_nope_g, mla_rope_g, nsa_qk_g,
           nsa_cmp_pe, nsa_cmp_w1, nsa_cmp_b1, nsa_cmp_w2, mem_norm_g, w_mem_kv, mem_qk_g, w_out, rel_bias):
    N, T, _ = x_prompt.shape
    DB = x_sample.shape[0]
    depth = norm_g.shape[0]
    n_pool = cache_mla.shape[1]
    npg = page_table.shape[1]
    past = npg * PAGE
    wb = cache_nsa_win.shape[2]
    assert x_sample.shape[1] == 1 and T % TQ == 0 and past % TQ == 0 and T // SEL_BLOCK <= LANES
    assert DB % 8 == 0 and wb == min(WINDOW, past) and wb >= REL_MAX_DIST and past // SEL_BLOCK >= 2
    npg_p = T // PAGE
    n_past_blk = past // SEL_BLOCK

    tf = TF if T % TF == 0 else TQ
    bias = _bias_tables(rel_bias, tf)
    bd = jnp.stack([bias[:, 0:tf], bias[:, tf:2 * tf]], axis=1)
    dcn = bias[:, 2 * tf:2 * tf + TQ, 0:LANES]
    dvec = bias[:, 2 * tf + TQ, 0:REL_MAX_DIST + 1]
    dvec = dvec.at[:, REL_MAX_DIST].set(0.0)
    hg = jnp.arange(NSA_G)[:, None] * NSA_HPG + jnp.minimum(jnp.arange(8), NSA_HPG - 1)[None, :]
    dg = dvec[hg]
    r64 = jnp.arange(SEL_BLOCK)
    z64 = jnp.zeros((NSA_G, 8, SEL_BLOCK), F32)
    b1 = dg[:, :, SEL_BLOCK - r64]
    b2 = dg[:, :, 2 * SEL_BLOCK - r64]
    in_page = lambda b, blk: jnp.concatenate([b, z64] if blk % (PAGE // SEL_BLOCK) == 0 else [z64, b], -1)
    bt = jnp.stack([in_page(b1, n_past_blk - 1), in_page(b2, n_past_blk - 2)], axis=1)
    d0 = jnp.broadcast_to(dg[:, :, 0:1], (NSA_G, 8, LANES))
    bw = dg[:, :, jnp.minimum(wb - jnp.arange(wb), REL_MAX_DIST)]

    def feat_major(a):
        a = jnp.moveaxis(a, -4, -1) if a.ndim >= 5 else jnp.swapaxes(a, -1, -2)
        return a.reshape(a.shape[:-4] + (-1, a.shape[-1])) if a.ndim >= 5 else a
    mla_t = feat_major(cache_mla).reshape(depth * n_pool, MLA_KVL + MLA_ROPE, PAGE)
    cmp_t = feat_major(cache_nsa_cmp).reshape(depth * n_pool, 2 * LANES, PAGE)
    slc_t = feat_major(cache_nsa_slc).reshape(depth * n_pool, 2 * LANES, PAGE)

    cos_p, sin_p = _rope_tables(jnp.arange(T))
    cos_s, sin_s = _rope_tables(jnp.full((DB,), past))
    pt_prompt = jnp.arange(N * npg_p, dtype=jnp.int32).reshape(N, npg_p)
    tm_p = TQ

    yp = x_prompt.reshape(N * T, D_MODEL)
    ys = x_sample.reshape(DB, D_MODEL)
    outs = [[] for _ in range(9)]
    for l in range(depth):
        lw = _layer_weights(l, norm_g, w_in, mla_q_norm, mla_w_uq, mla_kv_norm, mla_w_ukv, mla_nope_g, mla_rope_g,
                            nsa_qk_g, nsa_cmp_pe, nsa_cmp_w1, nsa_cmp_b1, nsa_cmp_w2, mem_norm_g, w_mem_kv,
                            mem_qk_g, w_out)
        p = _prep(yp, cos_p, sin_p, lw, tm_p)
        r3 = lambda a: a.reshape(N, T, a.shape[-1])
        o_a = _flash("mla", r3(p["qa"]), r3(p["ka"]), r3(p["va"]))
        kc4, vc4 = _compress(pt_prompt, p["rows_cmp"].reshape(N * npg_p, PAGE, 2 * LANES), lw, npg_p, False)
        o_c, sel, _ = _cmp_topk(r3(p["qb"]), kc4, vc4, dcn, TQ, 0, T // SEL_BLOCK)
        o_s = _flash("slc", r3(p["qb"]), r3(p["ks4"]), r3(p["vs4"]), sel=sel, bd=bd)
        o_w = _flash("win", r3(p["qb"]), r3(p["kw4"]), r3(p["vw4"]), bd=bd)
        mkv = _mem_kv(mem_prompt, lw)
        o_m = _mem_attn(r3(p["qm"]), mkv, tm_p)
        f2 = lambda a: a.reshape(N * T, a.shape[-1])
        yp = _merge(yp, f2(o_a), f2(o_c), f2(o_s), f2(o_w), p["gb"], f2(o_m), p["gates"], lw, tm_p)
        outs[0].append(p["rows_a"].reshape(N, T, MLA_KVL + MLA_ROPE))
        outs[2].append(p["rows_cmp"].reshape(N, T, 2, NSA_G, HD))
        outs[4].append(p["rows_slc"].reshape(N, T, 2, NSA_G, HD))
        outs[6].append(p["rows_win"].reshape(N, T, 2, NSA_G, HD)[:, T - min(WINDOW, T):])
        outs[8].append(mkv.reshape(N, mkv.shape[1], 2, MEM_H, HD))
        ps = _prep(ys, cos_s, sin_s, lw, DB)
        pt_l = page_table + l * n_pool
        qa3 = ps["qa"].reshape(DB, MLA_H, LANES)
        q6 = (qa3[:, :, None, :MLA_NOPE] * jnp.eye(MLA_H, dtype=BF16)[None, :, :, None]).reshape(DB, MLA_H, MLA_H * HD)
        q6 = jnp.pad(q6, ((0, 0), (0, 8 - MLA_H), (0, 0)))
        qr = jnp.pad(qa3[:, :, MLA_NOPE:MLA_QK], ((0, 0), (0, 8 - MLA_H), (0, LANES - MLA_ROPE)))
        own8 = lambda a: jnp.pad(a[:, None, :], ((0, 0), (0, 7), (0, 0)))
        oa_s = _mla_decode(pt_l, mla_t, q6, qr, own8(ps["rows_a"]), lw, npg)[:, 0]
        kc4s, vc4s = _compress(pt_l, cmp_t, lw, npg, True)
        qb8 = jnp.pad(ps["qb"][:, None, :], ((0, 0), (0, 7), (0, 0)))
        oc_s, _, idx = _cmp_topk(qb8, kc4s, vc4s, dcn[:, 0:8], 8, past, n_past_blk + 1)
        qg = jnp.pad(ps["qb"].astype(F32).reshape(DB, NSA_G, NSA_HPG, HD),
                     ((0, 0), (0, 0), (0, 8 - NSA_HPG), (0, 0))).reshape(DB, NSA_G * 8, HD)
        os_s = _slc_decode_t(idx[:, 0, 0:NSA_G * TOP_N], pt_l, slc_t, qg, own8(ps["rows_slc"]), bt, d0, n_past_blk)
        ow_s = _win_decode_t(feat_major(cache_nsa_win[l]), qg, own8(ps["rows_win"]), bw, d0, wb - WINDOW + 1)
        qmh = ps["qm"].reshape(DB, MEM_H, HD)
        qm8 = (qmh[:, :, None, :] * jnp.eye(MEM_H, dtype=BF16)[None, :, :, None]).reshape(DB, MEM_H, MEM_H * HD)
        qm8 = jnp.pad(qm8, ((0, 0), (0, 8 - MEM_H), (0, 0)))
        om8 = _mem_decode_t(feat_major(cache_mem_kv[l]), qm8)
        ys = _merge(ys, oa_s, oc_s[:, 0], os_s[:, 0], ow_s[:, 0], ps["gb"], om8[:, 0], ps["gates"], lw, DB)
        outs[1].append(ps["rows_a"].reshape(DB, 1, MLA_KVL + MLA_ROPE))
        outs[3].append(ps["rows_cmp"].reshape(DB, 1, 2, NSA_G, HD))
        outs[5].append(ps["rows_slc"].reshape(DB, 1, 2, NSA_G, HD))
        new_win = jnp.concatenate([cache_nsa_win[l][:, 1:], ps["rows_win"].reshape(DB, 1, 2, NSA_G, HD)], axis=1)
        outs[7].append(new_win)
    return (yp.reshape(N, T, D_MODEL), ys.reshape(DB, 1, D_MODEL)) + tuple(jnp.stack(o) for o in outs)
```

```python
import functools
import math

import numpy as np
import jax
import jax.numpy as jnp
from jax import lax
from jax.experimental import pallas as pl
from jax.experimental.pallas import tpu as pltpu

F32 = jnp.float32
BF16 = jnp.bfloat16

D_MODEL = 1024
PAGE = 128
HD = 64
MLA_H = 6
MLA_QL = 256
MLA_KVL = 128
MLA_NOPE = 64
MLA_ROPE = 32
MLA_QK = MLA_NOPE + MLA_ROPE
ROPE_THETA = 10000.0
NSA_H = 6
NSA_G = 2
NSA_HPG = 3
CMP_BLOCK = 32
CMP_STRIDE = 16
CMP_HID = 128
SEL_BLOCK = 64
TOP_N = 16
WINDOW = 512
FORCE_BONUS = 1000.0
MEM_H = 4
REL_BUCKETS = 32
REL_MAX_DIST = 128
EPS = 1e-6
NEG = -1e30
M_INIT = -1e29
PAD_IMP = -3e38

LANES = 128
TQ = 256
TF = 512
CPAD = 16
VMEM_LIMIT = 56 * 1024 * 1024

C_QL, C_KVL, C_KR, C_NQ, C_NKV, C_NG, C_MQ, C_GATE, W_IN_P = 0, 256, 384, 512, 896, 1664, 1792, 2048, 3072


def _cp(sem):
    return pltpu.CompilerParams(dimension_semantics=sem, vmem_limit_bytes=VMEM_LIMIT)


def _dot(a, b):
    return jnp.dot(a, b, preferred_element_type=F32)


def _dot_nt(a, b):
    return lax.dot_general(a, b, (((1,), (1,)), ((), ())), preferred_element_type=F32)


def _split_bf16(y):
    hi = y.astype(BF16)
    lo = (y - hi.astype(F32)).astype(BF16)
    return jnp.concatenate([hi, lo], axis=1)


def _seg_ms(y, s2):
    return _dot(_split_bf16(y * y), s2)


def _lane(shape):
    return lax.broadcasted_iota(jnp.int32, shape, len(shape) - 1)


def _row(shape):
    return lax.broadcasted_iota(jnp.int32, shape, len(shape) - 2)


def _rel_bucket_np(d):
    n = np.maximum(d, 0)
    exact = REL_BUCKETS // 2
    logp = np.log(np.maximum(n, 1).astype(np.float32) / np.float32(exact)) / np.float32(math.log(REL_MAX_DIST / exact))
    large = np.minimum(exact + (logp * np.float32(REL_BUCKETS - exact)).astype(np.int32), REL_BUCKETS - 1)
    b = np.where(n < exact, n, large)
    return np.where((d < 0) | (d >= REL_MAX_DIST), REL_BUCKETS - 1, b).astype(np.int32)


def _bucket_tables(tf):
    i = np.arange(tf)[:, None]
    j = np.arange(tf)[None, :]
    diag = _rel_bucket_np(i - j)
    near = _rel_bucket_np(tf + i - j)
    i = np.arange(TQ)[:, None]
    cmpn = _rel_bucket_np(np.where(j < 2 * CPAD, i + (TQ - CMP_BLOCK + 1) - CMP_STRIDE * j, -1))
    vec = _rel_bucket_np(np.broadcast_to(j, (8, tf)))
    return np.concatenate([diag, near, cmpn, vec], axis=0)


def _seg_mats():
    s64 = np.kron(np.eye(2), np.ones((64, 64))) / 64.0
    sqa = np.zeros((128, 128))
    sqa[:64, :64] = 1.0 / 64.0
    sqa[64:96, 64:96] = 1.0 / 32.0
    out = np.stack([np.concatenate([s64, s64], 0), np.concatenate([sqa, sqa], 0)])
    return jnp.asarray(out, BF16)


def _seg_mat_heads(nh):
    s = np.kron(np.eye(nh), np.ones((64, 64))) / 64.0
    return jnp.asarray(np.concatenate([s, s], 0), BF16)


def _gate_expand():
    e = np.zeros((128, 3 * NSA_H * HD))
    for h in range(NSA_H):
        for b in range(3):
            e[h * 3 + b, b * NSA_H * HD + h * HD:b * NSA_H * HD + (h + 1) * HD] = 1.0
    return jnp.asarray(np.concatenate([e, e], 0), BF16)


def _cmp_to_sel(n_c, n_sel, n_sel_pad):
    c = np.arange(n_c)[:, None] * CMP_STRIDE
    j = np.arange(n_sel_pad)[None, :] * SEL_BLOCK
    m = ((c < j + SEL_BLOCK) & (c + CMP_BLOCK > j) & (np.arange(n_c)[:, None] < n_c - 1)
         & (np.arange(n_sel_pad)[None, :] < n_sel)).astype(np.float64)
    out = np.zeros((n_c + 2 * CPAD, n_sel_pad))
    out[CPAD:CPAD + n_c] = m
    return jnp.asarray(out, BF16)


def _bias_kernel(tbl_ref, bk_ref, o_ref):
    h = pl.program_id(0)
    bk = bk_ref[...]
    base = tbl_ref[REL_BUCKETS - 1, h]
    acc = jnp.zeros(bk.shape, F32)
    for b in range(REL_BUCKETS - 1):
        acc = jnp.where(bk == b, tbl_ref[b, h] - base, acc)
    o_ref[0] = acc


def _bias_tables(rel_bias, tf):
    bk = jnp.asarray(_bucket_tables(tf))
    rows = bk.shape[0]
    return pl.pallas_call(
        _bias_kernel,
        grid=(NSA_H,),
        in_specs=[pl.BlockSpec(memory_space=pltpu.SMEM),
                  pl.BlockSpec((rows, tf), lambda h: (0, 0))],
        out_specs=pl.BlockSpec((1, rows, tf), lambda h: (h, 0, 0)),
        out_shape=jax.ShapeDtypeStruct((NSA_H, rows, tf), F32),
        compiler_params=_cp(("arbitrary",)),
        name="bias_tables",
    )(rel_bias, bk)


def _rope(y, cos, sin):
    lane = _lane(y.shape)
    rot = jnp.where(lane < MLA_NOPE + MLA_ROPE // 2, pltpu.roll(y, LANES - MLA_ROPE // 2, 1),
                    pltpu.roll(y, MLA_ROPE // 2, 1))
    return y * cos + rot * sin


def _four_variants(y):
    lo = _lane(y.shape) < HD
    r = pltpu.roll(y, HD, 1)
    z = jnp.zeros_like(y)
    return [jnp.where(lo, y, z), jnp.where(lo, z, r), jnp.where(lo, r, z), jnp.where(lo, z, y)]


def _with_ones(v, parity):
    return jnp.where(_lane(v.shape) == (HD if parity == 0 else 0), 1.0, v)


def _prep_kernel(x_ref, cos_ref, sin_ref, gv_ref, win_ref, wuq_ref, wuk_ref, wuv_ref, s2_ref,
                 qa_ref, rowsa_ref, ka_ref, va_ref, qb_ref, cmp_ref, slc_ref, wn_ref, gb_ref, qm_ref,
                 gates_ref, ks4_ref, vs4_ref, kw4_ref, vw4_ref):
    x = x_ref[...]
    cos = cos_ref[...]
    sin = sin_ref[...]
    s64 = s2_ref[0]
    sqa = s2_ref[1]

    def gv(i, w=LANES):
        return gv_ref[i:i + 1, 0:w]

    xn = (x * lax.rsqrt(jnp.mean(x * x, axis=-1, keepdims=True) + EPS) * gv(0, D_MODEL)).astype(BF16)

    def proj(a, b):
        return _dot(xn, win_ref[:, a:b])

    ql = proj(C_QL, C_QL + MLA_QL)
    qln = (ql * lax.rsqrt(jnp.mean(ql * ql, axis=-1, keepdims=True) + EPS) * gv(1, MLA_QL)).astype(BF16)
    q = _dot(qln, wuq_ref[...])
    for h in range(MLA_H):
        y = q[:, h * LANES:(h + 1) * LANES]
        yn = y * lax.rsqrt(_seg_ms(y, sqa) + EPS) * gv(3)
        qa_ref[:, h * LANES:(h + 1) * LANES] = _rope(yn, cos, sin).astype(BF16)
    kvl = proj(C_KVL, C_KVL + MLA_KVL)
    cn = kvl * lax.rsqrt(jnp.mean(kvl * kvl, axis=-1, keepdims=True) + EPS) * gv(2)
    krs = proj(C_KR, C_KR + LANES)
    krr = _rope(krs * lax.rsqrt(_seg_ms(krs, sqa) + EPS) * gv(4), cos, sin)
    rowsa_ref[:, 0:MLA_KVL] = cn
    rowsa_ref[:, MLA_KVL:MLA_KVL + MLA_ROPE] = krr[:, MLA_NOPE:MLA_NOPE + MLA_ROPE]
    cb = cn.astype(BF16)
    kraw = _dot(cb, wuk_ref[...])
    for h in range(MLA_H):
        y = kraw[:, h * LANES:(h + 1) * LANES]
        kn = y * lax.rsqrt(_seg_ms(y, s64) + EPS) * gv(5)
        ka_ref[:, h * LANES:(h + 1) * LANES] = (kn + krr).astype(BF16)
    va = _dot(cb, wuv_ref[...])
    for h in range(MLA_H):
        va_ref[:, h * LANES:(h + 1) * LANES] = _with_ones(va[:, h * LANES:(h + 1) * LANES], h % 2).astype(BF16)
    nq = proj(C_NQ, C_NQ + NSA_H * HD)
    for v in range(NSA_H * HD // LANES):
        y = nq[:, v * LANES:(v + 1) * LANES]
        qb_ref[:, v * LANES:(v + 1) * LANES] = (y * lax.rsqrt(_seg_ms(y, s64) + EPS) * gv(6)).astype(BF16)
    nkv = proj(C_NKV, C_NKV + 6 * LANES)
    cmp_ref[...] = nkv[:, 0:2 * LANES]
    for base, gi, rows_ref, k4_ref, v4_ref in ((2 * LANES, 7, slc_ref, ks4_ref, vs4_ref),
                                               (4 * LANES, 8, wn_ref, kw4_ref, vw4_ref)):
        k = nkv[:, base:base + LANES]
        kn = k * lax.rsqrt(_seg_ms(k, s64) + EPS) * gv(gi)
        vv = nkv[:, base + LANES:base + 2 * LANES]
        rows_ref[:, 0:LANES] = kn
        rows_ref[:, LANES:2 * LANES] = vv
        for i, (a, b) in enumerate(zip(_four_variants(kn), _four_variants(vv))):
            k4_ref[:, i * LANES:(i + 1) * LANES] = a.astype(BF16)
            v4_ref[:, i * LANES:(i + 1) * LANES] = _with_ones(b, i % 2).astype(BF16)
    ng = proj(C_NG, C_NG + LANES)
    gb_ref[...] = 1.0 / (1.0 + jnp.exp(-ng))
    mq = proj(C_MQ, C_MQ + MEM_H * HD)
    for v in range(MEM_H * HD // LANES):
        y = mq[:, v * LANES:(v + 1) * LANES]
        qm_ref[:, v * LANES:(v + 1) * LANES] = (y * lax.rsqrt(_seg_ms(y, s64) + EPS) * gv(9)).astype(BF16)
    gt = proj(C_GATE, W_IN_P)
    gates_ref[...] = gt / (1.0 + jnp.exp(-gt))


_PREP_OUT = (("qa", 768, BF16), ("rows_a", 160, F32), ("ka", 768, BF16), ("va", 768, BF16), ("qb", 384, BF16),
             ("rows_cmp", 256, F32), ("rows_slc", 256, F32), ("rows_win", 256, F32), ("gb", 128, F32),
             ("qm", 256, BF16), ("gates", 1024, F32), ("ks4", 512, BF16), ("vs4", 512, BF16),
             ("kw4", 512, BF16), ("vw4", 512, BF16))


def _prep(x2, cos, sin, lw, tm):
    R = x2.shape[0]
    n_tab = cos.shape[0] // tm
    full = lambda a: pl.BlockSpec(a.shape, lambda i: (0,) * a.ndim)
    outs = pl.pallas_call(
        _prep_kernel,
        grid=(R // tm,),
        in_specs=[pl.BlockSpec((tm, D_MODEL), lambda i: (i, 0)),
                  pl.BlockSpec((tm, LANES), lambda i: (i % n_tab, 0)),
                  pl.BlockSpec((tm, LANES), lambda i: (i % n_tab, 0)),
                  full(lw["gv"]), full(lw["w_in"]), full(lw["w_uq"]), full(lw["w_uk_p"]), full(lw["w_uv_p"]),
                  full(lw["s2"])],
        out_specs=[pl.BlockSpec((tm, w), lambda i: (i, 0)) for _, w, _ in _PREP_OUT],
        out_shape=[jax.ShapeDtypeStruct((R, w), dt) for _, w, dt in _PREP_OUT],
        compiler_params=_cp(("arbitrary",)),
        name="prep",
    )(x2, cos, sin, lw["gv"], lw["w_in"], lw["w_uq"], lw["w_uk_p"], lw["w_uv_p"], lw["s2"])
    return {name: o for (name, _, _), o in zip(_PREP_OUT, outs)}


def _flash_kernel(*refs, mode, t):
    if mode == "mla":
        q_ref, k_ref, v_ref, o_ref, m_scr, acc_scr = refs
        sel_ref = bd_ref = None
    elif mode == "slc":
        q_ref, sel_ref, k_ref, v_ref, bd_ref, o_ref, m_scr, acc_scr = refs
    else:
        q_ref, k_ref, v_ref, bd_ref, o_ref, m_scr, acc_scr = refs
        sel_ref = None
    qi = pl.program_id(1)
    m_scr[...] = jnp.full(m_scr.shape, M_INIT, F32)
    acc_scr[...] = jnp.zeros(acc_scr.shape, F32)

    def tile(j, dist):
        rows = pl.ds(pl.multiple_of(j * t, t), t)
        if mode == "slc":
            blk = (j * t + _row((t, LANES))) // SEL_BLOCK
            onehot = (_lane((t, LANES)) == blk).astype(BF16)
        if dist is not None:
            d = dist * t + _row((t, t)) - _lane((t, t))
        for h in range(NSA_H):
            pair = (h // 2) * LANES
            if mode == "mla":
                qh = q_ref[0, :, h * LANES:(h + 1) * LANES]
                kh = k_ref[0, rows, h * LANES:(h + 1) * LANES]
                vh = v_ref[0, rows, h * LANES:(h + 1) * LANES]
            else:
                gp = ((h // NSA_HPG) * 2 + h % 2) * LANES
                qh = q_ref[0, :, pair:pair + LANES]
                kh = k_ref[0, rows, gp:gp + LANES]
                vh = v_ref[0, rows, gp:gp + LANES]
                if mode == "slc":
                    g = h // NSA_HPG
                    qh = jnp.concatenate([qh, sel_ref[0, :, g * LANES:(g + 1) * LANES]], axis=1)
                    kh = jnp.concatenate([kh, onehot], axis=1)
            s = _dot_nt(qh, kh)
            if dist is not None:
                if mode != "mla" and dist <= 1:
                    s = s + bd_ref[h, dist]
                if dist == 0:
                    s = jnp.where(d >= 0, s, NEG)
                elif mode == "win" and (dist + 1) * t > WINDOW:
                    s = jnp.where(d < WINDOW, s, NEG)
            m_prev = m_scr[h]
            m_new = jnp.maximum(m_prev, jnp.max(s, axis=1, keepdims=True))
            alpha = jnp.exp(m_prev - m_new)
            p = jnp.exp(s - m_new[:, 0:1])
            acc_scr[h] = alpha * acc_scr[h] + _dot(p.astype(BF16), vh)
            m_scr[h] = m_new

    def far(j, c):
        tile(j, None)
        return c

    if mode == "mla":
        lax.fori_loop(0, qi, far, 0)
    elif mode == "slc":
        lax.fori_loop(0, jnp.maximum(qi - 1, 0), far, 0)
    near = 1 if mode == "slc" else (-(-WINDOW // t) if mode == "win" else 0)
    for dist in range(near, 0, -1):
        @pl.when(qi >= dist)
        def _(dist=dist):
            tile(qi - dist, dist)
    tile(qi, 0)

    lo = _lane((t, LANES)) < HD
    for pp in range(NSA_H // 2):
        ae = acc_scr[2 * pp]
        ao = acc_scr[2 * pp + 1]
        ra = ae / jnp.maximum(ae[:, HD:HD + 1], 1e-30)
        rb = ao / jnp.maximum(ao[:, 0:1], 1e-30)
        o_ref[0, :, pp * LANES:(pp + 1) * LANES] = jnp.where(lo, ra, rb)


def _flash(mode, q, k, v, sel=None, bd=None):
    N, T, wq = q.shape
    t = bd.shape[-1] if bd is not None else (TF if T % TF == 0 else TQ)
    nq = T // t
    TQ_ = t
    whole = lambda a: pl.BlockSpec((1,) + a.shape[1:], lambda n, i: (n, 0, 0), pipeline_mode=pl.Buffered(1))
    args = [q]
    specs = [pl.BlockSpec((1, TQ_, wq), lambda n, i: (n, i, 0))]
    if mode == "slc":
        args.append(sel)
        specs.append(pl.BlockSpec((1, TQ_, sel.shape[2]), lambda n, i: (n, i, 0)))
    args += [k, v]
    specs += [whole(k), whole(v)]
    if mode != "mla":
        args.append(bd)
        specs.append(pl.BlockSpec(bd.shape, lambda n, i: (0, 0, 0, 0), pipeline_mode=pl.Buffered(1)))
    return pl.pallas_call(
        functools.partial(_flash_kernel, mode=mode, t=t),
        grid=(N, nq),
        in_specs=specs,
        out_specs=pl.BlockSpec((1, t, NSA_H * HD), lambda n, i: (n, i, 0)),
        out_shape=jax.ShapeDtypeStruct((N, T, NSA_H * HD), F32),
        scratch_shapes=[pltpu.VMEM((NSA_H, t, LANES), F32)] * 2,
        compiler_params=_cp(("arbitrary", "arbitrary")),
        name="flash_" + mode,
    )(*args)


def _gather_pages(cache_ref, pt_ref, seq, buf, sem, slot, npg, start, halves=False):
    for j in range(npg):
        rows = pl.ds(j * PAGE, PAGE)
        if halves:
            cps = [pltpu.make_async_copy(cache_ref.at[pt_ref[seq, j], :, pl.ds(hf * LANES, LANES)],
                                         buf.at[slot, hf, rows], sem.at[slot]) for hf in range(2)]
        else:
            cps = [pltpu.make_async_copy(cache_ref.at[pt_ref[seq, j]], buf.at[slot, rows], sem.at[slot])]
        for cp in cps:
            if start:
                cp.start()
            else:
                cp.wait()


def _gather_pages_t(cache_ref, pt_ref, seq, buf, sem, slot, npg, start):
    for j in range(npg):
        cp = pltpu.make_async_copy(cache_ref.at[pt_ref[seq, j]], buf.at[slot, j], sem.at[slot])
        if start:
            cp.start()
        else:
            cp.wait()


def _compress_kernel(pt_ref, cache_ref, w1_ref, pe_ref, b1_ref, w2_ref, gk_ref, kc4_ref, vc4_ref, *scratch,
                     npg, transposed):
    s = pl.program_id(0)
    ns = pl.num_programs(0)
    slot = s % 2
    L = npg * PAGE
    n_c = L // CMP_STRIDE
    if transposed:
        buft, rb, first_scr, second_scr, sem = scratch
        gather = functools.partial(_gather_pages_t, cache_ref, pt_ref, buf=buft, sem=sem, npg=npg)
    else:
        buf, first_scr, second_scr, sem = scratch
        gather = functools.partial(_gather_pages, cache_ref, pt_ref, buf=buf, sem=sem, npg=npg, halves=True)

    @pl.when(s == 0)
    def _():
        gather(seq=0, slot=0, start=True)

    @pl.when(s + 1 < ns)
    def _():
        gather(seq=s + 1, slot=1 - slot, start=True)

    gather(seq=s, slot=slot, start=False)
    if transposed:
        def to_rows(j, c):
            for hf in range(2):
                rb[hf, pl.ds(pl.multiple_of(j * PAGE, PAGE), PAGE), :] = buft[slot, j, hf * LANES:(hf + 1) * LANES, :].T
            return c
        lax.fori_loop(0, npg, to_rows, 0, unroll=4 if npg % 4 == 0 else 1)
        rows_of = lambda hf, ds: rb[hf, ds, :]
    else:
        rows_of = lambda hf, ds: buf[slot, hf, ds, :]
    zeros16 = jnp.zeros((CPAD, 4 * LANES), BF16)
    kc4_ref[0, 0:CPAD, :] = zeros16
    kc4_ref[0, CPAD + n_c:2 * CPAD + n_c, :] = zeros16
    vc4_ref[0, 0:CPAD, :] = zeros16
    vc4_ref[0, CPAD + n_c:2 * CPAD + n_c, :] = zeros16

    rc = min(n_c, 256)
    npos = 4
    half = CMP_STRIDE * HD
    for c0 in range(0, n_c, rc):
        acc = [jnp.zeros((rc, 2 * CMP_HID), F32) for _ in range(4)]
        for p0 in range(0, CMP_STRIDE, npos):
            xs = [[rows_of(hf, pl.ds(c0 * CMP_STRIDE + p0 + i, rc, stride=CMP_STRIDE)) for i in range(npos)]
                  for hf in range(2)]
            for sg in range(4):
                lhs = jnp.concatenate([x[:, (sg % 2) * HD:(sg % 2 + 1) * HD] for x in xs[sg // 2]],
                                      axis=1).astype(BF16)
                acc[sg] = acc[sg] + _dot(lhs, w1_ref[sg // 2, p0 * HD:(p0 + npos) * HD, :])
        for sg in range(4):
            first_scr[sg, c0:c0 + rc, :] = acc[sg][:, 0:CMP_HID]
            second_scr[sg, c0:c0 + rc, :] = acc[sg][:, CMP_HID:2 * CMP_HID]
    second_scr[:, n_c:n_c + 8, :] = jnp.zeros((4, 8, CMP_HID), F32)
    for c0 in range(0, n_c, rc):
        outs = []
        for sg in range(4):
            sidx = sg // 2
            b = (_dot(pe_ref[sidx][:, 0:half], w1_ref[sidx])[0:1, 0:CMP_HID]
                 + _dot(pe_ref[sidx][:, half:2 * half], w1_ref[sidx])[0:1, CMP_HID:2 * CMP_HID] + b1_ref[sidx][0:1, :])
            hcur = first_scr[sg, c0:c0 + rc, :] + second_scr[sg, pl.ds(c0 + 1, rc), :] + b
            hcur = 0.5 * hcur * (1.0 + jnp.tanh(0.7978845608028654 * (hcur + 0.044715 * hcur * hcur * hcur)))
            outs.append(_dot(hcur.astype(BF16), w2_ref[sidx]))
        z = jnp.zeros((rc, HD), F32)
        kn = [o * lax.rsqrt(jnp.mean(o * o, axis=-1, keepdims=True) + EPS) * gk_ref[...] for o in outs[0:2]]
        rows = slice(CPAD + c0, CPAD + c0 + rc)
        for i, (g, p) in enumerate(((0, 0), (0, 1), (1, 0), (1, 1))):
            kk = jnp.concatenate([kn[g], z] if p == 0 else [z, kn[g]], axis=1)
            vv = jnp.concatenate([outs[2 + g], z] if p == 0 else [z, outs[2 + g]], axis=1)
            kc4_ref[0, rows, i * LANES:(i + 1) * LANES] = kk.astype(BF16)
            vc4_ref[0, rows, i * LANES:(i + 1) * LANES] = vv.astype(BF16)


def _compress(pt, cache, lw, npg, transposed):
    n_seq = pt.shape[0]
    L = npg * PAGE
    n_c = L // CMP_STRIDE
    rows = n_c + 2 * CPAD
    full = lambda a: pl.BlockSpec(a.shape, lambda s, pt_: (0,) * a.ndim)
    ws = [lw["cmp_w1"], lw["cmp_pe"], lw["cmp_b1"], lw["cmp_w2"], lw["cmp_gk"]]
    if transposed:
        scratch = [pltpu.VMEM((2, npg, 2 * LANES, PAGE), F32), pltpu.VMEM((2, L, LANES), F32)]
    else:
        scratch = [pltpu.VMEM((2, 2, L, LANES), F32)]
    scratch += [pltpu.VMEM((4, n_c + 8, CMP_HID), F32)] * 2
    return pl.pallas_call(
        functools.partial(_compress_kernel, npg=npg, transposed=transposed),
        grid_spec=pltpu.PrefetchScalarGridSpec(
            num_scalar_prefetch=1,
            grid=(n_seq,),
            in_specs=[pl.BlockSpec(memory_space=pl.ANY)] + [full(a) for a in ws],
            out_specs=[pl.BlockSpec((1, rows, 4 * LANES), lambda s, pt_: (s, 0, 0))] * 2,
            scratch_shapes=scratch + [pltpu.SemaphoreType.DMA((2,))]),
        out_shape=[jax.ShapeDtypeStruct((n_seq, rows, 4 * LANES), BF16)] * 2,
        compiler_params=_cp(("arbitrary",)),
        name="compress",
    )(pt, cache, *ws)


def _cmp_topk_kernel(q_ref, kc4_ref, vc4_ref, mm_ref, dcn_ref, oc_ref, sel_ref, *rest, tq, qpos0, n_c, n_sel, topk):
    idx_ref = rest[0] if topk else None
    i = pl.program_id(1)
    q0 = qpos0 + i * tq
    c0p = pl.multiple_of((q0 // TQ) * CPAD, CPAD)
    nsp = mm_ref.shape[1]
    qpos = q0 + _row((tq, 1))
    wn = 2 * CPAD
    colf = _lane((tq, n_c))
    valid_far = colf < (c0p - CPAD)
    c_near = c0p - CPAD + _lane((tq, wn))
    valid_near = (c_near >= 0) & (c_near * CMP_STRIDE + (CMP_BLOCK - 1) <= qpos)
    far_rows = slice(CPAD, CPAD + n_c)
    near_rows = pl.ds(c0p, wn)
    idx_acc = jnp.zeros((tq, LANES), jnp.int32)
    res = [None] * NSA_H
    for g in range(NSA_G):
        ps_far = jnp.zeros((tq, n_c), F32)
        ps_near = jnp.zeros((tq, wn), F32)
        for hp in range(NSA_HPG):
            h = g * NSA_HPG + hp
            gp = (g * 2 + h % 2) * LANES
            qh = q_ref[0, :, (h // 2) * LANES:(h // 2 + 1) * LANES]
            s_far = jnp.where(valid_far, _dot_nt(qh, kc4_ref[0, far_rows, gp:gp + LANES]), NEG)
            s_near = _dot_nt(qh, kc4_ref[0, near_rows, gp:gp + LANES]) + dcn_ref[h][:, 0:wn]
            s_near = jnp.where(valid_near, s_near, NEG)
            m = jnp.maximum(jnp.maximum(jnp.max(s_far, axis=1, keepdims=True), jnp.max(s_near, axis=1, keepdims=True)),
                            M_INIT)
            pf = jnp.exp(s_far - m)
            pn = jnp.exp(s_near - m)
            inv = 1.0 / jnp.maximum(jnp.sum(pf, axis=1, keepdims=True) + jnp.sum(pn, axis=1, keepdims=True), 1e-30)
            pf = pf * inv
            pn = pn * inv
            res[h] = (_dot(pf.astype(BF16), vc4_ref[0, far_rows, gp:gp + LANES])
                      + _dot(pn.astype(BF16), vc4_ref[0, near_rows, gp:gp + LANES]))
            ps_far = ps_far + pf
            ps_near = ps_near + pn
        mf = mm_ref[far_rows, :]
        mn = mm_ref[near_rows, :]
        imp = (_dot(_split_bf16(ps_far), jnp.concatenate([mf, mf], axis=0))
               + _dot(_split_bf16(ps_near), jnp.concatenate([mn, mn], axis=0)))
        jl = _lane((tq, nsp))
        cur = qpos // SEL_BLOCK
        forced = (jl == 0) | (jl == cur) | (jl == cur - 1)
        valid = jl * SEL_BLOCK <= qpos
        imp = jnp.where(valid, imp + jnp.where(forced, FORCE_BONUS, 0.0), NEG)
        imp = jnp.where(jl < n_sel, imp, PAD_IMP)
        if topk:
            sel, idx_acc = _topk_rounds(imp, g, idx_acc)
            sel_ref[0, :, g * nsp:(g + 1) * nsp] = jnp.where(sel, 0.0, NEG).astype(BF16)
        else:
            sel_ref[0, :, g * nsp:(g + 1) * nsp] = imp
    for pp in range(NSA_H // 2):
        oc_ref[0, :, pp * LANES:(pp + 1) * LANES] = res[2 * pp] + res[2 * pp + 1]
    if topk:
        idx_ref[0] = idx_acc


def _topk_rounds(imp, g, idx_acc):
    jf = _lane(imp.shape).astype(F32)
    lane_i = _lane(idx_acc.shape)
    sel = jnp.zeros(imp.shape, jnp.bool_)
    for kk in range(TOP_N):
        mx = jnp.max(imp, axis=1, keepdims=True)
        am = jnp.min(jnp.where(imp == mx, jf, 1e9), axis=1, keepdims=True)
        hit = jf == am
        sel = sel | hit
        imp = jnp.where(hit, -jnp.inf, imp)
        idx_acc = jnp.where(lane_i == g * TOP_N + kk, am.astype(jnp.int32), idx_acc)
    return sel, idx_acc


def _cmp_topk(qb, kc4, vc4, dcn, tq, qpos0, n_sel, topk):
    n_seq, T, _ = qb.shape
    rows = kc4.shape[1]
    n_c = rows - 2 * CPAD
    nsp = -(-n_sel // LANES) * LANES
    mm = _cmp_to_sel(n_c, n_sel, nsp)
    whole = lambda a: pl.BlockSpec((1,) + a.shape[1:], lambda n, i: (n, 0, 0))
    out_specs = [pl.BlockSpec((1, tq, NSA_H * HD), lambda n, i: (n, i, 0)),
                 pl.BlockSpec((1, tq, NSA_G * nsp), lambda n, i: (n, i, 0))]
    out_shape = [jax.ShapeDtypeStruct((n_seq, T, NSA_H * HD), F32),
                 jax.ShapeDtypeStruct((n_seq, T, NSA_G * nsp), BF16 if topk else F32)]
    if topk:
        out_specs.append(pl.BlockSpec((1, tq, LANES), lambda n, i: (n, i, 0)))
        out_shape.append(jax.ShapeDtypeStruct((n_seq, T, LANES), jnp.int32))
    return pl.pallas_call(
        functools.partial(_cmp_topk_kernel, tq=tq, qpos0=qpos0, n_c=n_c, n_sel=n_sel, topk=topk),
        grid=(n_seq, T // tq),
        in_specs=[pl.BlockSpec((1, tq, NSA_H * HD), lambda n, i: (n, i, 0)), whole(kc4), whole(vc4),
                  pl.BlockSpec(mm.shape, lambda n, i: (0, 0)),
                  pl.BlockSpec(dcn.shape, lambda n, i: (0, 0, 0))],
        out_specs=out_specs,
        out_shape=out_shape,
        compiler_params=_cp(("arbitrary", "arbitrary")),
        name="cmp_topk" if topk else "cmp_attn",
    )(qb, kc4, vc4, mm, dcn)


def _topk_kernel(imp_ref, idx_ref, *, nsp):
    idx_acc = jnp.zeros(idx_ref.shape, jnp.int32)
    for g in range(NSA_G):
        _, idx_acc = _topk_rounds(imp_ref[:, g * nsp:(g + 1) * nsp], g, idx_acc)
    idx_ref[...] = idx_acc


def _topk(imp, nsp):
    R = imp.shape[0]
    return pl.pallas_call(
        functools.partial(_topk_kernel, nsp=nsp),
        grid=(1,),
        in_specs=[pl.BlockSpec(imp.shape, lambda i: (0, 0))],
        out_specs=pl.BlockSpec((R, LANES), lambda i: (0, 0)),
        out_shape=jax.ShapeDtypeStruct((R, LANES), jnp.int32),
        compiler_params=_cp(("arbitrary",)),
        name="topk",
    )(imp)


def _mem_kv_kernel(mem_ref, g_ref, w_ref, s2_ref, gk_ref, o_ref):
    x = mem_ref[0]
    xn = (x * lax.rsqrt(jnp.mean(x * x, axis=-1, keepdims=True) + EPS) * g_ref[...]).astype(BF16)
    kv = _dot(xn, w_ref[...])
    nk = MEM_H * HD
    for v in range(nk // LANES):
        y = kv[:, v * LANES:(v + 1) * LANES]
        o_ref[0, :, v * LANES:(v + 1) * LANES] = y * lax.rsqrt(_seg_ms(y, s2_ref[0]) + EPS) * gk_ref[...]
    o_ref[0, :, nk:2 * nk] = kv[:, nk:2 * nk]


def _mem_kv(mem, lw):
    N, M, _ = mem.shape
    full = lambda a: pl.BlockSpec(a.shape, lambda n: (0,) * a.ndim)
    ws = [lw["mem_norm_g"], lw["w_mem_kv"], lw["s2"], lw["mem_gk"]]
    return pl.pallas_call(
        _mem_kv_kernel,
        grid=(N,),
        in_specs=[pl.BlockSpec((1, M, D_MODEL), lambda n: (n, 0, 0))] + [full(a) for a in ws],
        out_specs=pl.BlockSpec((1, M, 2 * MEM_H * HD), lambda n: (n, 0, 0)),
        out_shape=jax.ShapeDtypeStruct((N, M, 2 * MEM_H * HD), F32),
        compiler_params=_cp(("arbitrary",)),
        name="mem_kv",
    )(mem, *ws)


def _mem_attn_kernel(q_ref, kv_ref, o_ref):
    nk = MEM_H * HD
    tm = q_ref.shape[1]
    lo = _lane((kv_ref.shape[1], LANES)) < HD
    lo_o = _lane((tm, LANES)) < HD
    for pp in range(MEM_H // 2):
        qp = q_ref[0, :, pp * LANES:(pp + 1) * LANES]
        kp = kv_ref[0, :, pp * LANES:(pp + 1) * LANES]
        vp = kv_ref[0, :, nk + pp * LANES:nk + (pp + 1) * LANES].astype(BF16)
        r = []
        for par in range(2):
            kz = jnp.where(lo if par == 0 else ~lo, kp, 0.0).astype(BF16)
            s = _dot_nt(qp, kz)
            p = jnp.exp(s - jnp.max(s, axis=1, keepdims=True))
            r.append(_dot(p.astype(BF16), vp) / jnp.sum(p, axis=1, keepdims=True))
        o_ref[0, :, pp * LANES:(pp + 1) * LANES] = jnp.where(lo_o, r[0], r[1])


def _mem_attn(qm, mkv, tm):
    N, T, w = qm.shape
    return pl.pallas_call(
        _mem_attn_kernel,
        grid=(N, T // tm),
        in_specs=[pl.BlockSpec((1, tm, w), lambda n, i: (n, i, 0)),
                  pl.BlockSpec((1,) + mkv.shape[1:], lambda n, i: (n, 0, 0))],
        out_specs=pl.BlockSpec((1, tm, w), lambda n, i: (n, i, 0)),
        out_shape=jax.ShapeDtypeStruct((N, T, w), F32),
        compiler_params=_cp(("arbitrary", "arbitrary")),
        name="mem_attn",
    )(qm, mkv)


def _mla_decode_kernel(pt_ref, cache_ref, q6_ref, qr_ref, own_ref, wuk_ref, wukt_ref, wuv_ref, gk_ref, o_ref,
                       buf, sem, *, npg, pc):
    s = pl.program_id(0)
    ns = pl.num_programs(0)
    slot = s % 2
    w = MLA_H * HD

    @pl.when(s == 0)
    def _():
        _gather_pages_t(cache_ref, pt_ref, 0, buf, sem, 0, npg, True)

    @pl.when(s + 1 < ns)
    def _():
        _gather_pages_t(cache_ref, pt_ref, s + 1, buf, sem, 1 - slot, npg, True)

    _gather_pages_t(cache_ref, pt_ref, s, buf, sem, slot, npg, False)
    q6g = q6_ref[0].astype(F32) * gk_ref[...]
    qabs = _dot_nt(q6g.astype(BF16), wuk_ref[...]).astype(BF16)
    qr = qr_ref[0][:, 0:MLA_ROPE]
    rowi = _row((8, pc * PAGE))

    def body(i, carry):
        m_prev, l_prev, acc = carry
        ct = jnp.concatenate([buf[slot, i * pc + j, 0:MLA_KVL, :] for j in range(pc)], axis=1).astype(BF16)
        krt = jnp.concatenate([buf[slot, i * pc + j, MLA_KVL:MLA_KVL + MLA_ROPE, :] for j in range(pc)],
                              axis=1).astype(BF16)
        kraw = _dot(wukt_ref[...], ct)
        sq = kraw * kraw
        ms = jnp.zeros((8, pc * PAGE), F32)
        for h in range(MLA_H):
            ms = jnp.where(rowi == h, jnp.sum(sq[h * HD:(h + 1) * HD], axis=0, keepdims=True), ms)
        sc = _dot(qabs, ct) * lax.rsqrt(ms * (1.0 / HD) + EPS) + _dot(qr, krt)
        m_new = jnp.maximum(m_prev, jnp.max(sc, axis=1, keepdims=True))
        alpha = jnp.exp(m_prev - m_new)
        p = jnp.exp(sc - m_new)
        return (m_new, alpha * l_prev + jnp.sum(p, axis=1, keepdims=True),
                alpha * acc + _dot_nt(p.astype(BF16), ct))

    carry = (jnp.full((8, 1), M_INIT, F32), jnp.zeros((8, 1), F32), jnp.zeros((8, MLA_KVL), F32))
    m_prev, l_prev, acc = lax.fori_loop(0, npg // pc, body, carry)
    own = own_ref[0]
    c_own = own[0:1, 0:MLA_KVL]
    kraw_o = _dot(own[:, 0:MLA_KVL].astype(BF16), wuk_ref[...])[0:1, :]
    own_head = (_lane((8, w)) // HD) == _row((8, w))
    ms_o = jnp.sum(jnp.where(own_head, kraw_o * kraw_o, 0.0), axis=1, keepdims=True)
    s_own = (jnp.sum(q6g * kraw_o, axis=1, keepdims=True) * lax.rsqrt(ms_o * (1.0 / HD) + EPS)
             + jnp.sum(qr.astype(F32) * own[0:1, MLA_KVL:MLA_KVL + MLA_ROPE], axis=1, keepdims=True))
    m_new = jnp.maximum(m_prev, s_own)
    alpha = jnp.exp(m_prev - m_new)
    p_own = jnp.exp(s_own - m_new)
    l_fin = alpha * l_prev + p_own
    acc = alpha * acc + p_own * c_own
    o_lat = (acc / jnp.maximum(l_fin, 1e-30)).astype(BF16)
    res = _dot(o_lat, wuv_ref[...])
    o_ref[0] = jnp.broadcast_to(jnp.sum(jnp.where(own_head, res, 0.0), axis=0, keepdims=True), (8, w))


def _mla_decode(pt, cache, q6, qr, own, lw, npg):
    DB = pt.shape[0]
    pc = next(c for c in (16, 8, 4, 2, 1) if npg % c == 0)
    full = lambda a: pl.BlockSpec(a.shape, lambda s, pt_: (0,) * a.ndim)
    per = lambda a: pl.BlockSpec((1,) + a.shape[1:], lambda s, pt_: (s, 0, 0))
    ws = [lw["w_uk"], lw["w_uk_t"], lw["w_uv"], lw["mla_gk"]]
    w = MLA_H * HD
    return pl.pallas_call(
        functools.partial(_mla_decode_kernel, npg=npg, pc=pc),
        grid_spec=pltpu.PrefetchScalarGridSpec(
            num_scalar_prefetch=1,
            grid=(DB,),
            in_specs=[pl.BlockSpec(memory_space=pl.ANY), per(q6), per(qr), per(own)] + [full(a) for a in ws],
            out_specs=pl.BlockSpec((1, 8, w), lambda s, pt_: (s, 0, 0)),
            scratch_shapes=[pltpu.VMEM((2, npg, MLA_KVL + MLA_ROPE, PAGE), F32), pltpu.SemaphoreType.DMA((2,))]),
        out_shape=jax.ShapeDtypeStruct((DB, 8, w), F32),
        compiler_params=_cp(("arbitrary",)),
        name="mla_decode",
    )(pt, cache, q6, qr, own, *ws)


def _decode_core(q8, keys, bias, valid, own_row, own_bias, own_flag):
    kb = keys.astype(BF16)
    s = _dot_nt(q8, kb) + bias
    if valid is not None:
        s = jnp.where(valid, s, NEG)
    s_own = jnp.sum(q8.astype(F32) * own_row, axis=1, keepdims=True) + own_bias
    s_own = jnp.where(own_flag, s_own, NEG)
    m = jnp.maximum(jnp.maximum(jnp.max(s, axis=1, keepdims=True), s_own), M_INIT)
    p = jnp.exp(s - m)
    p_own = jnp.exp(s_own - m)
    l = jnp.sum(p, axis=1, keepdims=True) + p_own
    return (_dot(p.astype(BF16), kb) + p_own * own_row) / jnp.maximum(l, 1e-30)


def _place_heads(res):
    v0 = res[0][:, 2 * HD:4 * HD]
    v1 = res[1][:, 2 * HD:4 * HD]
    v0r = pltpu.roll(v0, HD, 1)
    v1r = pltpu.roll(v1, HD, 1)
    lo = _lane((1, LANES)) < HD
    row = jnp.concatenate([jnp.where(lo, v0[0:1], v0r[1:2]), jnp.where(lo, v0[2:3], v1[0:1]),
                           jnp.where(lo, v1r[1:2], v1[2:3])], axis=1)
    return jnp.broadcast_to(row, (8, NSA_H * HD))


def _slc_decode_kernel(idx_ref, pt_ref, cache_ref, q8_ref, own_ref, bt_ref, d0_ref, o_ref, buf, sem, *, n_past_blk):
    s = pl.program_id(0)
    ns = pl.num_programs(0)
    slot = s % 2
    nblk = NSA_G * TOP_N
    bpp = PAGE // SEL_BLOCK

    def copies(seq, sl, start):
        for j in range(nblk):
            b = jnp.minimum(idx_ref[seq, j], n_past_blk - 1)
            src = pt_ref[seq, b // bpp] * bpp + b % bpp
            cp = pltpu.make_async_copy(cache_ref.at[src], buf.at[sl, pl.ds(j * SEL_BLOCK, SEL_BLOCK)], sem.at[sl])
            if start:
                cp.start()
            else:
                cp.wait()

    @pl.when(s == 0)
    def _():
        copies(0, 0, True)

    @pl.when(s + 1 < ns)
    def _():
        copies(s + 1, 1 - slot, True)

    copies(s, slot, False)
    own = own_ref[0][0:1, :]
    lo = _lane((8, LANES)) < HD
    nk = TOP_N * SEL_BLOCK
    res = []
    for g in range(NSA_G):
        pieces, vpieces = [], []
        own_flag = idx_ref[s, g * TOP_N] == n_past_blk
        for jj in range(TOP_N // 2):
            ia = idx_ref[s, g * TOP_N + 2 * jj]
            ib = idx_ref[s, g * TOP_N + 2 * jj + 1]
            own_flag = own_flag | (ia == n_past_blk) | (ib == n_past_blk)
            z = jnp.zeros((8, LANES), F32)
            bias = (jnp.where(ia == n_past_blk - 1, bt_ref[g, 0], z) + jnp.where(ia == n_past_blk - 2, bt_ref[g, 2], z)
                    + jnp.where(ib == n_past_blk - 1, bt_ref[g, 1], z) + jnp.where(ib == n_past_blk - 2, bt_ref[g, 3], z))
            pieces.append(bias)
            va = jnp.where(ia < n_past_blk, 1, 0)
            vb = jnp.where(ib < n_past_blk, 1, 0)
            vpieces.append(jnp.where(lo, va, vb) > 0)
        keys = buf[slot, g * nk:(g + 1) * nk, :]
        res.append(_decode_core(q8_ref[0, g], keys, jnp.concatenate(pieces, axis=1),
                                jnp.concatenate(vpieces, axis=1), own, d0_ref[g][:, 0:1], own_flag))
    o_ref[0] = _place_heads(res)


def _slc_decode(idx, pt, cache, q8, own, bt, d0, n_past_blk):
    DB = idx.shape[0]
    per3 = lambda a: pl.BlockSpec((1,) + a.shape[1:], lambda s, i_, p_: (s,) + (0,) * (a.ndim - 1))
    full = lambda a: pl.BlockSpec(a.shape, lambda s, i_, p_: (0,) * a.ndim)
    return pl.pallas_call(
        functools.partial(_slc_decode_kernel, n_past_blk=n_past_blk),
        grid_spec=pltpu.PrefetchScalarGridSpec(
            num_scalar_prefetch=2,
            grid=(DB,),
            in_specs=[pl.BlockSpec(memory_space=pl.ANY), per3(q8), per3(own), full(bt), full(d0)],
            out_specs=pl.BlockSpec((1, 8, NSA_H * HD), lambda s, i_, p_: (s, 0, 0)),
            scratch_shapes=[pltpu.VMEM((2, NSA_G * TOP_N * SEL_BLOCK, 2 * LANES), F32), pltpu.SemaphoreType.DMA((2,))]),
        out_shape=jax.ShapeDtypeStruct((DB, 8, NSA_H * HD), F32),
        compiler_params=_cp(("arbitrary",)),
        name="slc_decode",
    )(idx, pt, cache, q8, own, bt, d0)


def _win_decode_kernel(wb_ref, q8_ref, own_ref, bw_ref, d0_ref, o_ref, *, first_valid):
    keys = wb_ref[0]
    n = keys.shape[0]
    own = own_ref[0][0:1, :]
    valid = _lane((8, n)) >= first_valid
    o_ref[0] = _place_heads([_decode_core(q8_ref[0, g], keys, bw_ref[g], valid, own, d0_ref[g][:, 0:1], True)
                             for g in range(NSA_G)])


def _win_decode(wbuf, q8, own, bw, d0, first_valid):
    DB, wb, _ = wbuf.shape
    per = lambda a: pl.BlockSpec((1,) + a.shape[1:], lambda s: (s,) + (0,) * (a.ndim - 1))
    full = lambda a: pl.BlockSpec(a.shape, lambda s: (0,) * a.ndim)
    return pl.pallas_call(
        functools.partial(_win_decode_kernel, first_valid=first_valid),
        grid=(DB,),
        in_specs=[per(wbuf), per(q8), per(own), full(bw), full(d0)],
        out_specs=pl.BlockSpec((1, 8, NSA_H * HD), lambda s: (s, 0, 0)),
        out_shape=jax.ShapeDtypeStruct((DB, 8, NSA_H * HD), F32),
        compiler_params=_cp(("arbitrary",)),
        name="win_decode",
    )(wbuf, q8, own, bw, d0)


def _mem_decode_kernel(kv_ref, q8_ref, o_ref):
    kb = kv_ref[0].astype(BF16)
    s = _dot_nt(q8_ref[0], kb)
    p = jnp.exp(s - jnp.max(s, axis=1, keepdims=True))
    nk = MEM_H * HD
    res = (_dot(p.astype(BF16), kb) / jnp.sum(p, axis=1, keepdims=True))[:, nk:2 * nk]
    own_head = (_lane((8, nk)) // HD) == _row((8, nk))
    o_ref[0] = jnp.broadcast_to(jnp.sum(jnp.where(own_head, res, 0.0), axis=0, keepdims=True), (8, nk))


def _mem_decode(mkv, q8):
    DB, M, w = mkv.shape
    return pl.pallas_call(
        _mem_decode_kernel,
        grid=(DB,),
        in_specs=[pl.BlockSpec((1, M, w), lambda s: (s, 0, 0)), pl.BlockSpec((1, 8, w), lambda s: (s, 0, 0))],
        out_specs=pl.BlockSpec((1, 8, w // 2), lambda s: (s, 0, 0)),
        out_shape=jax.ShapeDtypeStruct((DB, 8, w // 2), F32),
        compiler_params=_cp(("arbitrary",)),
        name="mem_decode",
    )(mkv, q8)


def _decode_group(qg, kt, vt, bias, valid, k_own, v_own, own_bias, own_flag):
    s = _dot(qg.astype(BF16), kt.astype(BF16)) + bias
    s = jnp.where(valid, s, NEG)
    s_own = jnp.sum(qg * k_own, axis=1, keepdims=True) + own_bias
    s_own = jnp.where(own_flag, s_own, NEG)
    m = jnp.maximum(jnp.maximum(jnp.max(s, axis=1, keepdims=True), s_own), M_INIT)
    p = jnp.exp(s - m)
    p_own = jnp.exp(s_own - m)
    l = jnp.sum(p, axis=1, keepdims=True) + p_own
    return (_dot_nt(p.astype(BF16), vt.astype(BF16)) + p_own * v_own) / jnp.maximum(l, 1e-30)


def _head_row(res):
    d0 = jnp.concatenate([res[0], res[0]], axis=1)
    d1 = jnp.concatenate([res[1], res[1]], axis=1)
    lo = _lane((1, LANES)) < HD
    row = jnp.concatenate([jnp.where(lo, d0[0:1], d0[1:2]), jnp.where(lo, d0[2:3], d1[0:1]),
                           jnp.where(lo, d1[1:2], d1[2:3])], axis=1)
    return jnp.broadcast_to(row, (8, NSA_H * HD))


def _slc_decode_t_kernel(idx_ref, pt_ref, cache_ref, qg_ref, own_ref, bt_ref, d0_ref, o_ref, buf, sem, *, n_past_blk):
    s = pl.program_id(0)
    ns = pl.num_programs(0)
    slot = s % 2
    nblk = NSA_G * TOP_N
    bpp = PAGE // SEL_BLOCK

    def copies(seq, sl, start):
        for j in range(nblk):
            b = jnp.minimum(idx_ref[seq, j], n_past_blk - 1)
            cp = pltpu.make_async_copy(cache_ref.at[pt_ref[seq, b // bpp]], buf.at[sl, j], sem.at[sl])
            if start:
                cp.start()
            else:
                cp.wait()

    @pl.when(s == 0)
    def _():
        copies(0, 0, True)

    @pl.when(s + 1 < ns)
    def _():
        copies(s + 1, 1 - slot, True)

    copies(s, slot, False)
    own = own_ref[0][0:1, :]
    lane = _lane((8, PAGE))
    res = []
    for g in range(NSA_G):
        pieces, vpieces = [], []
        own_flag = idx_ref[s, g * TOP_N] == n_past_blk
        for j in range(TOP_N):
            ib = idx_ref[s, g * TOP_N + j]
            own_flag = own_flag | (ib == n_past_blk)
            z = jnp.zeros((8, PAGE), F32)
            pieces.append(jnp.where(ib == n_past_blk - 1, bt_ref[g, 0], z) + jnp.where(ib == n_past_blk - 2, bt_ref[g, 1], z))
            half = jnp.where(ib < n_past_blk, ib % bpp, bpp)
            vpieces.append((lane // SEL_BLOCK) == half)
        kt = jnp.concatenate([buf[slot, g * TOP_N + j, g * HD:(g + 1) * HD, :] for j in range(TOP_N)], axis=1)
        vt = jnp.concatenate([buf[slot, g * TOP_N + j, (NSA_G + g) * HD:(NSA_G + g + 1) * HD, :] for j in range(TOP_N)],
                             axis=1)
        res.append(_decode_group(qg_ref[0, g * 8:(g + 1) * 8, :], kt, vt, jnp.concatenate(pieces, axis=1),
                                 jnp.concatenate(vpieces, axis=1), own[:, g * HD:(g + 1) * HD],
                                 own[:, (NSA_G + g) * HD:(NSA_G + g + 1) * HD], d0_ref[g][:, 0:1], own_flag))
    o_ref[0] = _head_row(res)


def _slc_decode_t(idx, pt, cache, qg, own, bt, d0, n_past_blk):
    DB = idx.shape[0]
    per3 = lambda a: pl.BlockSpec((1,) + a.shape[1:], lambda s, i_, p_: (s,) + (0,) * (a.ndim - 1))
    full = lambda a: pl.BlockSpec(a.shape, lambda s, i_, p_: (0,) * a.ndim)
    return pl.pallas_call(
        functools.partial(_slc_decode_t_kernel, n_past_blk=n_past_blk),
        grid_spec=pltpu.PrefetchScalarGridSpec(
            num_scalar_prefetch=2,
            grid=(DB,),
            in_specs=[pl.BlockSpec(memory_space=pl.ANY), per3(qg), per3(own), full(bt), full(d0)],
            out_specs=pl.BlockSpec((1, 8, NSA_H * HD), lambda s, i_, p_: (s, 0, 0)),
            scratch_shapes=[pltpu.VMEM((2, NSA_G * TOP_N, 2 * LANES, PAGE), F32), pltpu.SemaphoreType.DMA((2,))]),
        out_shape=jax.ShapeDtypeStruct((DB, 8, NSA_H * HD), F32),
        compiler_params=_cp(("arbitrary",)),
        name="slc_decode",
    )(idx, pt, cache, qg, own, bt, d0)


def _win_decode_t_kernel(wt_ref, qg_ref, own_ref, bw_ref, d0_ref, o_ref, *, first_valid):
    n = wt_ref.shape[2]
    own = own_ref[0][0:1, :]
    valid = _lane((8, n)) >= first_valid
    res = [_decode_group(qg_ref[0, g * 8:(g + 1) * 8, :], wt_ref[0, g * HD:(g + 1) * HD, :],
                         wt_ref[0, (NSA_G + g) * HD:(NSA_G + g + 1) * HD, :], bw_ref[g], valid,
                         own[:, g * HD:(g + 1) * HD], own[:, (NSA_G + g) * HD:(NSA_G + g + 1) * HD],
                         d0_ref[g][:, 0:1], True) for g in range(NSA_G)]
    o_ref[0] = _head_row(res)


def _win_decode_t(wt, qg, own, bw, d0, first_valid):
    DB = wt.shape[0]
    per = lambda a: pl.BlockSpec((1,) + a.shape[1:], lambda s: (s,) + (0,) * (a.ndim - 1))
    full = lambda a: pl.BlockSpec(a.shape, lambda s: (0,) * a.ndim)
    return pl.pallas_call(
        functools.partial(_win_decode_t_kernel, first_valid=first_valid),
        grid=(DB,),
        in_specs=[per(wt), per(qg), per(own), full(bw), full(d0)],
        out_specs=pl.BlockSpec((1, 8, NSA_H * HD), lambda s: (s, 0, 0)),
        out_shape=jax.ShapeDtypeStruct((DB, 8, NSA_H * HD), F32),
        compiler_params=_cp(("arbitrary",)),
        name="win_decode",
    )(wt, qg, own, bw, d0)


def _mem_decode_t_kernel(kvt_ref, q8_ref, o_ref):
    nk = MEM_H * HD
    s = _dot(q8_ref[0], kvt_ref[0, 0:nk, :].astype(BF16))
    p = jnp.exp(s - jnp.max(s, axis=1, keepdims=True))
    res = _dot_nt(p.astype(BF16), kvt_ref[0, nk:2 * nk, :].astype(BF16)) / jnp.sum(p, axis=1, keepdims=True)
    own_head = (_lane((8, nk)) // HD) == _row((8, nk))
    o_ref[0] = jnp.broadcast_to(jnp.sum(jnp.where(own_head, res, 0.0), axis=0, keepdims=True), (8, nk))


def _mem_decode_t(kvt, q8):
    DB, w, M = kvt.shape
    return pl.pallas_call(
        _mem_decode_t_kernel,
        grid=(DB,),
        in_specs=[pl.BlockSpec((1, w, M), lambda s: (s, 0, 0)), pl.BlockSpec((1, 8, w // 2), lambda s: (s, 0, 0))],
        out_specs=pl.BlockSpec((1, 8, w // 2), lambda s: (s, 0, 0)),
        out_shape=jax.ShapeDtypeStruct((DB, 8, w // 2), F32),
        compiler_params=_cp(("arbitrary",)),
        name="mem_decode",
    )(kvt, q8)


def _merge_kernel(x_ref, oa_ref, oc_ref, os_ref, ow_ref, gb_ref, om_ref, gates_ref, e2_ref, wout_ref, y_ref):
    w = NSA_H * HD
    ge = _dot(_split_bf16(gb_ref[...]), e2_ref[...])
    ob = oc_ref[...] * ge[:, 0:w] + os_ref[...] * ge[:, w:2 * w] + ow_ref[...] * ge[:, 2 * w:3 * w]
    z = jnp.concatenate([oa_ref[...], ob, om_ref[...]], axis=1) * gates_ref[...]
    y_ref[...] = x_ref[...] + _dot(z.astype(BF16), wout_ref[...])


def _merge(x2, oa, oc, osl, ow, gb, om, gates, lw, tm):
    R = x2.shape[0]
    row = lambda a: pl.BlockSpec((tm, a.shape[1]), lambda i: (i, 0))
    full = lambda a: pl.BlockSpec(a.shape, lambda i: (0,) * a.ndim)
    acts = [x2, oa, oc, osl, ow, gb, om, gates]
    return pl.pallas_call(
        _merge_kernel,
        grid=(R // tm,),
        in_specs=[row(a) for a in acts] + [full(lw["e2"]), full(lw["w_out"])],
        out_specs=pl.BlockSpec((tm, D_MODEL), lambda i: (i, 0)),
        out_shape=jax.ShapeDtypeStruct((R, D_MODEL), F32),
        compiler_params=_cp(("arbitrary",)),
        name="merge",
    )(*acts, lw["e2"], lw["w_out"])


def _layer_weights(l, norm_g, w_in, mla_q_norm, mla_w_uq, mla_kv_norm, mla_w_ukv, mla_nope_g, mla_rope_g,
                   nsa_qk_g, nsa_cmp_pe, nsa_cmp_w1, nsa_cmp_b1, nsa_cmp_w2, mem_norm_g, w_mem_kv, mem_qk_g, w_out):
    w = w_in[l]
    o = np.cumsum((0, 256, 128, 32, 384, 384, 768, 18, 384, 256, 256))
    z = lambda n: jnp.zeros((D_MODEL, n), F32)
    w_in_p = jnp.concatenate([
        w[:, o[0]:o[1]], w[:, o[1]:o[2]],
        z(MLA_NOPE), w[:, o[2]:o[3]], z(LANES - MLA_NOPE - MLA_ROPE),
        w[:, o[4]:o[5]], w[:, o[5]:o[6]],
        w[:, o[6]:o[7]], z(LANES - 18),
        w[:, o[8]:o[9]],
        w[:, o[3]:o[4]], w[:, o[7]:o[8]], w[:, o[9]:o[10]]], axis=1).astype(BF16)
    uq = mla_w_uq[l].reshape(MLA_QL, MLA_H, MLA_QK)
    w_uq = jnp.pad(uq, ((0, 0), (0, 0), (0, LANES - MLA_QK))).reshape(MLA_QL, MLA_H * LANES).astype(BF16)
    ukv = mla_w_ukv[l].reshape(MLA_KVL, MLA_H, 2 * HD)
    w_uk = ukv[:, :, :HD].reshape(MLA_KVL, MLA_H * HD).astype(BF16)
    w_uk_p = jnp.pad(ukv[:, :, :HD], ((0, 0), (0, 0), (0, HD))).reshape(MLA_KVL, MLA_H * LANES).astype(BF16)
    w_uv = ukv[:, :, HD:].reshape(MLA_KVL, MLA_H * HD).astype(BF16)
    uv2 = ukv[:, :, HD:].reshape(MLA_KVL, MLA_H // 2, 2, HD)
    zv = jnp.zeros_like(uv2[:, :, 0])
    w_uv_p = jnp.stack([uv2[:, :, 0], zv, zv, uv2[:, :, 1]], axis=2).reshape(MLA_KVL, MLA_H * LANES).astype(BF16)
    sa = MLA_QK ** -0.5
    sb = HD ** -0.5
    pad = lambda v: jnp.pad(v, (0, D_MODEL - v.shape[0]))
    z32 = jnp.zeros((32,), F32)
    z64 = jnp.zeros((64,), F32)
    qk = nsa_qk_g[l]
    gv = jnp.stack([
        norm_g[l], pad(mla_q_norm[l]), pad(mla_kv_norm[l]),
        pad(jnp.concatenate([mla_nope_g[l, 0] * sa, mla_rope_g[l, 0] * sa, z32])),
        pad(jnp.concatenate([z64, mla_rope_g[l, 1], z32])),
        pad(jnp.concatenate([mla_nope_g[l, 1], z64])),
        pad(jnp.tile(qk[0] * sb, 2)), pad(jnp.tile(qk[2], 2)), pad(jnp.tile(qk[3], 2)),
        pad(jnp.tile(mem_qk_g[l, 0] * sb, 2))] + [jnp.zeros((D_MODEL,), F32)] * 6)
    pe = nsa_cmp_pe[l].reshape(2, 1, CMP_BLOCK * HD)
    return dict(
        gv=gv, w_in=w_in_p, w_uq=w_uq, w_uk=w_uk, w_uk_t=w_uk.T, w_uk_p=w_uk_p, w_uv=w_uv, w_uv_p=w_uv_p, s2=_seg_mats(),
        mla_gk=jnp.tile(mla_nope_g[l, 1], MLA_H)[None, :],
        cmp_w1=jnp.concatenate([nsa_cmp_w1[l][:, :CMP_STRIDE * HD], nsa_cmp_w1[l][:, CMP_STRIDE * HD:]],
                               axis=-1).astype(BF16), cmp_pe=jnp.broadcast_to(pe, (2, 8, CMP_BLOCK * HD)).astype(BF16),
        cmp_b1=jnp.broadcast_to(nsa_cmp_b1[l][:, None, :], (2, 8, CMP_HID)),
        cmp_w2=nsa_cmp_w2[l].astype(BF16), cmp_gk=qk[1][None, :],
        mem_norm_g=mem_norm_g[l][None, :], w_mem_kv=w_mem_kv[l].astype(BF16),
        mem_gk=jnp.tile(mem_qk_g[l, 1], 2)[None, :],
        e2=_gate_expand(), w_out=w_out[l].astype(BF16))


def _rope_tables(pos):
    half = MLA_ROPE // 2
    inv = ROPE_THETA ** (-jnp.arange(half, dtype=F32) / half)
    ang = pos.astype(F32)[:, None] * inv[None, :]
    c, s = jnp.cos(ang), jnp.sin(ang)
    n = pos.shape[0]
    cos = jnp.concatenate([jnp.ones((n, MLA_NOPE), F32), c, c, jnp.zeros((n, LANES - MLA_QK), F32)], axis=1)
    sin = jnp.concatenate([jnp.zeros((n, MLA_NOPE), F32), -s, s, jnp.zeros((n, LANES - MLA_QK), F32)], axis=1)
    return cos, sin


def _group_queries(qb):
    DB = qb.shape[0]
    qh = qb.reshape(DB, NSA_G, NSA_HPG, HD)
    out = jnp.zeros((DB, NSA_G, 8, 4, HD), qb.dtype)
    for g in range(NSA_G):
        out = out.at[:, g, :NSA_HPG, g].set(qh[:, g])
    return out.reshape(DB, NSA_G, 8, 4 * HD)


def kernel(x_prompt, x_sample, mem_prompt, cache_mla, cache_nsa_cmp, cache_nsa_slc, cache_nsa_win, cache_mem_kv,
           page_table, norm_g, w_in, mla_q_norm, mla_w_uq, mla_kv_norm, mla_w_ukv, mla_nope_g, mla_rope_g, nsa_qk_g,
           nsa_cmp_pe, nsa_cmp_w1, nsa_cmp_b1, nsa_cmp_w2, mem_norm_g, w_mem_kv, mem_qk_g, w_out, rel_bias):
    N, T, _ = x_prompt.shape
    DB = x_sample.shape[0]
    depth = norm_g.shape[0]
    n_pool = cache_mla.shape[1]
    npg = page_table.shape[1]
    past = npg * PAGE
    wb = cache_nsa_win.shape[2]
    assert x_sample.shape[1] == 1 and T % TQ == 0 and past % TQ == 0 and T // SEL_BLOCK <= LANES
    assert DB % 8 == 0 and wb == min(WINDOW, past) and wb >= REL_MAX_DIST and past // SEL_BLOCK >= 2
    npg_p = T // PAGE
    n_past_blk = past // SEL_BLOCK

    tf = TF if T % TF == 0 else TQ
    bias = _bias_tables(rel_bias, tf)
    bd = jnp.stack([bias[:, 0:tf], bias[:, tf:2 * tf]], axis=1)
    dcn = bias[:, 2 * tf:2 * tf + TQ, 0:LANES]
    dvec = bias[:, 2 * tf + TQ, 0:REL_MAX_DIST + 1]
    dvec = dvec.at[:, REL_MAX_DIST].set(0.0)
    hg = jnp.arange(NSA_G)[:, None] * NSA_HPG + jnp.minimum(jnp.arange(8), NSA_HPG - 1)[None, :]
    dg = dvec[hg]
    r64 = jnp.arange(SEL_BLOCK)
    z64 = jnp.zeros((NSA_G, 8, SEL_BLOCK), F32)
    b1 = dg[:, :, SEL_BLOCK - r64]
    b2 = dg[:, :, 2 * SEL_BLOCK - r64]
    in_page = lambda b, blk: jnp.concatenate([b, z64] if blk % (PAGE // SEL_BLOCK) == 0 else [z64, b], -1)
    bt = jnp.stack([in_page(b1, n_past_blk - 1), in_page(b2, n_past_blk - 2)], axis=1)
    d0 = jnp.broadcast_to(dg[:, :, 0:1], (NSA_G, 8, LANES))
    bw = dg[:, :, jnp.minimum(wb - jnp.arange(wb), REL_MAX_DIST)]

    def feat_major(a):
        a = jnp.moveaxis(a, -4, -1) if a.ndim >= 5 else jnp.swapaxes(a, -1, -2)
        return a.reshape(a.shape[:-4] + (-1, a.shape[-1])) if a.ndim >= 5 else a
    mla_t = feat_major(cache_mla).reshape(depth * n_pool, MLA_KVL + MLA_ROPE, PAGE)
    cmp_t = feat_major(cache_nsa_cmp).reshape(depth * n_pool, 2 * LANES, PAGE)
    slc_t = feat_major(cache_nsa_slc).reshape(depth * n_pool, 2 * LANES, PAGE)

    cos_p, sin_p = _rope_tables(jnp.arange(T))
    cos_s, sin_s = _rope_tables(jnp.full((DB,), past))
    pt_prompt = jnp.arange(N * npg_p, dtype=jnp.int32).reshape(N, npg_p)
    tm_p = TQ

    yp = x_prompt.reshape(N * T, D_MODEL)
    ys = x_sample.reshape(DB, D_MODEL)
    outs = [[] for _ in range(9)]
    for l in range(depth):
        lw = _layer_weights(l, norm_g, w_in, mla_q_norm, mla_w_uq, mla_kv_norm, mla_w_ukv, mla_nope_g, mla_rope_g,
                            nsa_qk_g, nsa_cmp_pe, nsa_cmp_w1, nsa_cmp_b1, nsa_cmp_w2, mem_norm_g, w_mem_kv,
                            mem_qk_g, w_out)
        p = _prep(yp, cos_p, sin_p, lw, tm_p)
        r3 = lambda a: a.reshape(N, T, a.shape[-1])
        o_a = _flash("mla", r3(p["qa"]), r3(p["ka"]), r3(p["va"]))
        kc4, vc4 = _compress(pt_prompt, p["rows_cmp"].reshape(N * npg_p, PAGE, 2 * LANES), lw, npg_p, False)
        o_c, sel, _ = _cmp_topk(r3(p["qb"]), kc4, vc4, dcn, TQ, 0, T // SEL_BLOCK, True)
        o_s = _flash("slc", r3(p["qb"]), r3(p["ks4"]), r3(p["vs4"]), sel=sel, bd=bd)
        o_w = _flash("win", r3(p["qb"]), r3(p["kw4"]), r3(p["vw4"]), bd=bd)
        mkv = _mem_kv(mem_prompt, lw)
        o_m = _mem_attn(r3(p["qm"]), mkv, tm_p)
        f2 = lambda a: a.reshape(N * T, a.shape[-1])
        yp = _merge(yp, f2(o_a), f2(o_c), f2(o_s), f2(o_w), p["gb"], f2(o_m), p["gates"], lw, tm_p)
        outs[0].append(p["rows_a"].reshape(N, T, MLA_KVL + MLA_ROPE))
        outs[2].append(p["rows_cmp"].reshape(N, T, 2, NSA_G, HD))
        outs[4].append(p["rows_slc"].reshape(N, T, 2, NSA_G, HD))
        outs[6].append(p["rows_win"].reshape(N, T, 2, NSA_G, HD)[:, T - min(WINDOW, T):])
        outs[8].append(mkv.reshape(N, mkv.shape[1], 2, MEM_H, HD))
        ps = _prep(ys, cos_s, sin_s, lw, DB)
        pt_l = page_table + l * n_pool
        qa3 = ps["qa"].reshape(DB, MLA_H, LANES)
        q6 = (qa3[:, :, None, :MLA_NOPE] * jnp.eye(MLA_H, dtype=BF16)[None, :, :, None]).reshape(DB, MLA_H, MLA_H * HD)
        q6 = jnp.pad(q6, ((0, 0), (0, 8 - MLA_H), (0, 0)))
        qr = jnp.pad(qa3[:, :, MLA_NOPE:MLA_QK], ((0, 0), (0, 8 - MLA_H), (0, LANES - MLA_ROPE)))
        own8 = lambda a: jnp.pad(a[:, None, :], ((0, 0), (0, 7), (0, 0)))
        oa_s = _mla_decode(pt_l, mla_t, q6, qr, own8(ps["rows_a"]), lw, npg)[:, 0]
        kc4s, vc4s = _compress(pt_l, cmp_t, lw, npg, True)
        qb8 = jnp.pad(ps["qb"][:, None, :], ((0, 0), (0, 7), (0, 0)))
        oc_s, imp_s = _cmp_topk(qb8, kc4s, vc4s, dcn[:, 0:8], 8, past, n_past_blk + 1, False)
        idx = _topk(imp_s[:, 0], imp_s.shape[-1] // NSA_G)
        qg = jnp.pad(ps["qb"].astype(F32).reshape(DB, NSA_G, NSA_HPG, HD),
                     ((0, 0), (0, 0), (0, 8 - NSA_HPG), (0, 0))).reshape(DB, NSA_G * 8, HD)
        os_s = _slc_decode_t(idx[:, 0:NSA_G * TOP_N], pt_l, slc_t, qg, own8(ps["rows_slc"]), bt, d0, n_past_blk)
        ow_s = _win_decode_t(feat_major(cache_nsa_win[l]), qg, own8(ps["rows_win"]), bw, d0, wb - WINDOW + 1)
        qmh = ps["qm"].reshape(DB, MEM_H, HD)
        qm8 = (qmh[:, :, None, :] * jnp.eye(MEM_H, dtype=BF16)[None, :, :, None]).reshape(DB, MEM_H, MEM_H * HD)
        qm8 = jnp.pad(qm8, ((0, 0), (0, 8 - MEM_H), (0, 0)))
        om8 = _mem_decode_t(feat_major(cache_mem_kv[l]), qm8)
        ys = _merge(ys, oa_s, oc_s[:, 0], os_s[:, 0], ow_s[:, 0], ps["gb"], om8[:, 0], ps["gates"], lw, DB)
        outs[1].append(ps["rows_a"].reshape(DB, 1, MLA_KVL + MLA_ROPE))
        outs[3].append(ps["rows_cmp"].reshape(DB, 1, 2, NSA_G, HD))
        outs[5].append(ps["rows_slc"].reshape(DB, 1, 2, NSA_G, HD))
        new_win = jnp.concatenate([cache_nsa_win[l][:, 1:], ps["rows_win"].reshape(DB, 1, 2, NSA_G, HD)], axis=1)
        outs[7].append(new_win)
    return (yp.reshape(N, T, D_MODEL), ys.reshape(DB, 1, D_MODEL)) + tuple(jnp.stack(o) for o in outs)
```

```python
import functools
import math

import numpy as np
import jax
import jax.numpy as jnp
from jax import lax
from jax.experimental import pallas as pl
from jax.experimental.pallas import tpu as pltpu

F32 = jnp.float32
BF16 = jnp.bfloat16

D_MODEL = 1024
PAGE = 128
HD = 64
MLA_H = 6
MLA_QL = 256
MLA_KVL = 128
MLA_NOPE = 64
MLA_ROPE = 32
MLA_QK = MLA_NOPE + MLA_ROPE
ROPE_THETA = 10000.0
NSA_H = 6
NSA_G = 2
NSA_HPG = 3
CMP_BLOCK = 32
CMP_STRIDE = 16
CMP_HID = 128
SEL_BLOCK = 64
TOP_N = 16
WINDOW = 512
FORCE_BONUS = 1000.0
MEM_H = 4
REL_BUCKETS = 32
REL_MAX_DIST = 128
EPS = 1e-6
NEG = -1e30
M_INIT = -1e29
PAD_IMP = -3e38

LANES = 128
TQ = 256
TF = 512
CPAD = 16
VMEM_LIMIT = 56 * 1024 * 1024

C_QL, C_KVL, C_KR, C_NQ, C_NKV, C_NG, C_MQ, C_GATE, W_IN_P = 0, 256, 384, 512, 896, 1664, 1792, 2048, 3072


def _cp(sem):
    return pltpu.CompilerParams(dimension_semantics=sem, vmem_limit_bytes=VMEM_LIMIT)


def _dot(a, b):
    return jnp.dot(a, b, preferred_element_type=F32)


def _dot_nt(a, b):
    return lax.dot_general(a, b, (((1,), (1,)), ((), ())), preferred_element_type=F32)


def _split_bf16(y):
    hi = y.astype(BF16)
    lo = (y - hi.astype(F32)).astype(BF16)
    return jnp.concatenate([hi, lo], axis=1)


def _seg_ms(y, s2):
    return _dot(_split_bf16(y * y), s2)


def _lane(shape):
    return lax.broadcasted_iota(jnp.int32, shape, len(shape) - 1)


def _row(shape):
    return lax.broadcasted_iota(jnp.int32, shape, len(shape) - 2)


def _rel_bucket_np(d):
    n = np.maximum(d, 0)
    exact = REL_BUCKETS // 2
    logp = np.log(np.maximum(n, 1).astype(np.float32) / np.float32(exact)) / np.float32(math.log(REL_MAX_DIST / exact))
    large = np.minimum(exact + (logp * np.float32(REL_BUCKETS - exact)).astype(np.int32), REL_BUCKETS - 1)
    b = np.where(n < exact, n, large)
    return np.where((d < 0) | (d >= REL_MAX_DIST), REL_BUCKETS - 1, b).astype(np.int32)


def _bucket_tables(tf):
    i = np.arange(tf)[:, None]
    j = np.arange(tf)[None, :]
    diag = _rel_bucket_np(i - j)
    near = _rel_bucket_np(tf + i - j)
    i = np.arange(TQ)[:, None]
    cmpn = _rel_bucket_np(np.where(j < 2 * CPAD, i + (TQ - CMP_BLOCK + 1) - CMP_STRIDE * j, -1))
    vec = _rel_bucket_np(np.broadcast_to(j, (8, tf)))
    return np.concatenate([diag, near, cmpn, vec], axis=0)


def _seg_mats():
    s64 = np.kron(np.eye(2), np.ones((64, 64))) / 64.0
    sqa = np.zeros((128, 128))
    sqa[:64, :64] = 1.0 / 64.0
    sqa[64:96, 64:96] = 1.0 / 32.0
    out = np.stack([np.concatenate([s64, s64], 0), np.concatenate([sqa, sqa], 0)])
    return jnp.asarray(out, BF16)


def _seg_mat_heads(nh):
    s = np.kron(np.eye(nh), np.ones((64, 64))) / 64.0
    return jnp.asarray(np.concatenate([s, s], 0), BF16)


def _gate_expand():
    e = np.zeros((128, 3 * NSA_H * HD))
    for h in range(NSA_H):
        for b in range(3):
            e[h * 3 + b, b * NSA_H * HD + h * HD:b * NSA_H * HD + (h + 1) * HD] = 1.0
    return jnp.asarray(np.concatenate([e, e], 0), BF16)


def _cmp_to_sel(n_c, n_sel, n_sel_pad):
    c = np.arange(n_c)[:, None] * CMP_STRIDE
    j = np.arange(n_sel_pad)[None, :] * SEL_BLOCK
    m = ((c < j + SEL_BLOCK) & (c + CMP_BLOCK > j) & (np.arange(n_c)[:, None] < n_c - 1)
         & (np.arange(n_sel_pad)[None, :] < n_sel)).astype(np.float64)
    out = np.zeros((n_c + 2 * CPAD, n_sel_pad))
    out[CPAD:CPAD + n_c] = m
    return jnp.asarray(out, BF16)


def _bias_kernel(tbl_ref, bk_ref, o_ref):
    h = pl.program_id(0)
    bk = bk_ref[...]
    base = tbl_ref[REL_BUCKETS - 1, h]
    acc = jnp.zeros(bk.shape, F32)
    for b in range(REL_BUCKETS - 1):
        acc = jnp.where(bk == b, tbl_ref[b, h] - base, acc)
    o_ref[0] = acc


def _bias_tables(rel_bias, tf):
    bk = jnp.asarray(_bucket_tables(tf))
    rows = bk.shape[0]
    return pl.pallas_call(
        _bias_kernel,
        grid=(NSA_H,),
        in_specs=[pl.BlockSpec(memory_space=pltpu.SMEM),
                  pl.BlockSpec((rows, tf), lambda h: (0, 0))],
        out_specs=pl.BlockSpec((1, rows, tf), lambda h: (h, 0, 0)),
        out_shape=jax.ShapeDtypeStruct((NSA_H, rows, tf), F32),
        compiler_params=_cp(("arbitrary",)),
        name="bias_tables",
    )(rel_bias, bk)


def _rope(y, cos, sin):
    lane = _lane(y.shape)
    rot = jnp.where(lane < MLA_NOPE + MLA_ROPE // 2, pltpu.roll(y, LANES - MLA_ROPE // 2, 1),
                    pltpu.roll(y, MLA_ROPE // 2, 1))
    return y * cos + rot * sin


def _four_variants(y):
    lo = _lane(y.shape) < HD
    r = pltpu.roll(y, HD, 1)
    z = jnp.zeros_like(y)
    return [jnp.where(lo, y, z), jnp.where(lo, z, r), jnp.where(lo, r, z), jnp.where(lo, z, y)]


def _with_ones(v, parity):
    return jnp.where(_lane(v.shape) == (HD if parity == 0 else 0), 1.0, v)


def _prep_kernel(x_ref, cos_ref, sin_ref, gv_ref, win_ref, wuq_ref, wuk_ref, wuv_ref, s2_ref,
                 qa_ref, rowsa_ref, ka_ref, va_ref, qb_ref, cmp_ref, slc_ref, wn_ref, gb_ref, qm_ref,
                 gates_ref, ks4_ref, vs4_ref, kw4_ref, vw4_ref):
    x = x_ref[...]
    cos = cos_ref[...]
    sin = sin_ref[...]
    s64 = s2_ref[0]
    sqa = s2_ref[1]

    def gv(i, w=LANES):
        return gv_ref[i:i + 1, 0:w]

    xn = (x * lax.rsqrt(jnp.mean(x * x, axis=-1, keepdims=True) + EPS) * gv(0, D_MODEL)).astype(BF16)

    def proj(a, b):
        return _dot(xn, win_ref[:, a:b])

    ql = proj(C_QL, C_QL + MLA_QL)
    qln = (ql * lax.rsqrt(jnp.mean(ql * ql, axis=-1, keepdims=True) + EPS) * gv(1, MLA_QL)).astype(BF16)
    q = _dot(qln, wuq_ref[...])
    for h in range(MLA_H):
        y = q[:, h * LANES:(h + 1) * LANES]
        yn = y * lax.rsqrt(_seg_ms(y, sqa) + EPS) * gv(3)
        qa_ref[:, h * LANES:(h + 1) * LANES] = _rope(yn, cos, sin).astype(BF16)
    kvl = proj(C_KVL, C_KVL + MLA_KVL)
    cn = kvl * lax.rsqrt(jnp.mean(kvl * kvl, axis=-1, keepdims=True) + EPS) * gv(2)
    krs = proj(C_KR, C_KR + LANES)
    krr = _rope(krs * lax.rsqrt(_seg_ms(krs, sqa) + EPS) * gv(4), cos, sin)
    rowsa_ref[:, 0:MLA_KVL] = cn
    rowsa_ref[:, MLA_KVL:MLA_KVL + MLA_ROPE] = krr[:, MLA_NOPE:MLA_NOPE + MLA_ROPE]
    cb = cn.astype(BF16)
    kraw = _dot(cb, wuk_ref[...])
    for h in range(MLA_H):
        y = kraw[:, h * LANES:(h + 1) * LANES]
        kn = y * lax.rsqrt(_seg_ms(y, s64) + EPS) * gv(5)
        ka_ref[:, h * LANES:(h + 1) * LANES] = (kn + krr).astype(BF16)
    va = _dot(cb, wuv_ref[...])
    for h in range(MLA_H):
        va_ref[:, h * LANES:(h + 1) * LANES] = _with_ones(va[:, h * LANES:(h + 1) * LANES], h % 2).astype(BF16)
    nq = proj(C_NQ, C_NQ + NSA_H * HD)
    for v in range(NSA_H * HD // LANES):
        y = nq[:, v * LANES:(v + 1) * LANES]
        qb_ref[:, v * LANES:(v + 1) * LANES] = (y * lax.rsqrt(_seg_ms(y, s64) + EPS) * gv(6)).astype(BF16)
    nkv = proj(C_NKV, C_NKV + 6 * LANES)
    cmp_ref[...] = nkv[:, 0:2 * LANES]
    for base, gi, rows_ref, k4_ref, v4_ref in ((2 * LANES, 7, slc_ref, ks4_ref, vs4_ref),
                                               (4 * LANES, 8, wn_ref, kw4_ref, vw4_ref)):
        k = nkv[:, base:base + LANES]
        kn = k * lax.rsqrt(_seg_ms(k, s64) + EPS) * gv(gi)
        vv = nkv[:, base + LANES:base + 2 * LANES]
        rows_ref[:, 0:LANES] = kn
        rows_ref[:, LANES:2 * LANES] = vv
        for i, (a, b) in enumerate(zip(_four_variants(kn), _four_variants(vv))):
            k4_ref[:, i * LANES:(i + 1) * LANES] = a.astype(BF16)
            v4_ref[:, i * LANES:(i + 1) * LANES] = _with_ones(b, i % 2).astype(BF16)
    ng = proj(C_NG, C_NG + LANES)
    gb_ref[...] = 1.0 / (1.0 + jnp.exp(-ng))
    mq = proj(C_MQ, C_MQ + MEM_H * HD)
    for v in range(MEM_H * HD // LANES):
        y = mq[:, v * LANES:(v + 1) * LANES]
        qm_ref[:, v * LANES:(v + 1) * LANES] = (y * lax.rsqrt(_seg_ms(y, s64) + EPS) * gv(9)).astype(BF16)
    gt = proj(C_GATE, W_IN_P)
    gates_ref[...] = gt / (1.0 + jnp.exp(-gt))


_PREP_OUT = (("qa", 768, BF16), ("rows_a", 160, F32), ("ka", 768, BF16), ("va", 768, BF16), ("qb", 384, BF16),
             ("rows_cmp", 256, F32), ("rows_slc", 256, F32), ("rows_win", 256, F32), ("gb", 128, F32),
             ("qm", 256, BF16), ("gates", 1024, F32), ("ks4", 512, BF16), ("vs4", 512, BF16),
             ("kw4", 512, BF16), ("vw4", 512, BF16))


def _prep(x2, cos, sin, lw, tm):
    R = x2.shape[0]
    n_tab = cos.shape[0] // tm
    full = lambda a: pl.BlockSpec(a.shape, lambda i: (0,) * a.ndim)
    outs = pl.pallas_call(
        _prep_kernel,
        grid=(R // tm,),
        in_specs=[pl.BlockSpec((tm, D_MODEL), lambda i: (i, 0)),
                  pl.BlockSpec((tm, LANES), lambda i: (i % n_tab, 0)),
                  pl.BlockSpec((tm, LANES), lambda i: (i % n_tab, 0)),
                  full(lw["gv"]), full(lw["w_in"]), full(lw["w_uq"]), full(lw["w_uk_p"]), full(lw["w_uv_p"]),
                  full(lw["s2"])],
        out_specs=[pl.BlockSpec((tm, w), lambda i: (i, 0)) for _, w, _ in _PREP_OUT],
        out_shape=[jax.ShapeDtypeStruct((R, w), dt) for _, w, dt in _PREP_OUT],
        compiler_params=_cp(("arbitrary",)),
        name="prep",
    )(x2, cos, sin, lw["gv"], lw["w_in"], lw["w_uq"], lw["w_uk_p"], lw["w_uv_p"], lw["s2"])
    return {name: o for (name, _, _), o in zip(_PREP_OUT, outs)}


def _flash_kernel(*refs, mode, t):
    if mode == "mla":
        q_ref, k_ref, v_ref, o_ref, m_scr, acc_scr = refs
        sel_ref = bd_ref = None
    elif mode == "slc":
        q_ref, sel_ref, k_ref, v_ref, bd_ref, o_ref, m_scr, acc_scr = refs
    else:
        q_ref, k_ref, v_ref, bd_ref, o_ref, m_scr, acc_scr = refs
        sel_ref = None
    qi = pl.program_id(1)
    m_scr[...] = jnp.full(m_scr.shape, M_INIT, F32)
    acc_scr[...] = jnp.zeros(acc_scr.shape, F32)

    def tile(j, dist):
        rows = pl.ds(pl.multiple_of(j * t, t), t)
        if mode == "slc":
            blk = (j * t + _row((t, LANES))) // SEL_BLOCK
            onehot = (_lane((t, LANES)) == blk).astype(BF16)
        if dist is not None:
            d = dist * t + _row((t, t)) - _lane((t, t))
        for h in range(NSA_H):
            pair = (h // 2) * LANES
            if mode == "mla":
                qh = q_ref[0, :, h * LANES:(h + 1) * LANES]
                kh = k_ref[0, rows, h * LANES:(h + 1) * LANES]
                vh = v_ref[0, rows, h * LANES:(h + 1) * LANES]
            else:
                gp = ((h // NSA_HPG) * 2 + h % 2) * LANES
                qh = q_ref[0, :, pair:pair + LANES]
                kh = k_ref[0, rows, gp:gp + LANES]
                vh = v_ref[0, rows, gp:gp + LANES]
                if mode == "slc":
                    g = h // NSA_HPG
                    qh = jnp.concatenate([qh, sel_ref[0, :, g * LANES:(g + 1) * LANES]], axis=1)
                    kh = jnp.concatenate([kh, onehot], axis=1)
            s = _dot_nt(qh, kh)
            if dist is not None:
                if mode != "mla" and dist <= 1:
                    s = s + bd_ref[h, dist]
                if dist == 0:
                    s = jnp.where(d >= 0, s, NEG)
                elif mode == "win" and (dist + 1) * t > WINDOW:
                    s = jnp.where(d < WINDOW, s, NEG)
            m_prev = m_scr[h]
            m_new = jnp.maximum(m_prev, jnp.max(s, axis=1, keepdims=True))
            alpha = jnp.exp(m_prev - m_new)
            p = jnp.exp(s - m_new[:, 0:1])
            acc_scr[h] = alpha * acc_scr[h] + _dot(p.astype(BF16), vh)
            m_scr[h] = m_new

    def far(j, c):
        tile(j, None)
        return c

    if mode == "mla":
        lax.fori_loop(0, qi, far, 0)
    elif mode == "slc":
        lax.fori_loop(0, jnp.maximum(qi - 1, 0), far, 0)
    near = 1 if mode == "slc" else (-(-WINDOW // t) if mode == "win" else 0)
    for dist in range(near, 0, -1):
        @pl.when(qi >= dist)
        def _(dist=dist):
            tile(qi - dist, dist)
    tile(qi, 0)

    lo = _lane((t, LANES)) < HD
    for pp in range(NSA_H // 2):
        ae = acc_scr[2 * pp]
        ao = acc_scr[2 * pp + 1]
        ra = ae / jnp.maximum(ae[:, HD:HD + 1], 1e-30)
        rb = ao / jnp.maximum(ao[:, 0:1], 1e-30)
        o_ref[0, :, pp * LANES:(pp + 1) * LANES] = jnp.where(lo, ra, rb)


def _flash(mode, q, k, v, sel=None, bd=None):
    N, T, wq = q.shape
    t = bd.shape[-1] if bd is not None else (TF if T % TF == 0 else TQ)
    nq = T // t
    TQ_ = t
    whole = lambda a: pl.BlockSpec((1,) + a.shape[1:], lambda n, i: (n, 0, 0), pipeline_mode=pl.Buffered(1))
    args = [q]
    specs = [pl.BlockSpec((1, TQ_, wq), lambda n, i: (n, i, 0))]
    if mode == "slc":
        args.append(sel)
        specs.append(pl.BlockSpec((1, TQ_, sel.shape[2]), lambda n, i: (n, i, 0)))
    args += [k, v]
    specs += [whole(k), whole(v)]
    if mode != "mla":
        args.append(bd)
        specs.append(pl.BlockSpec(bd.shape, lambda n, i: (0, 0, 0, 0), pipeline_mode=pl.Buffered(1)))
    return pl.pallas_call(
        functools.partial(_flash_kernel, mode=mode, t=t),
        grid=(N, nq),
        in_specs=specs,
        out_specs=pl.BlockSpec((1, t, NSA_H * HD), lambda n, i: (n, i, 0)),
        out_shape=jax.ShapeDtypeStruct((N, T, NSA_H * HD), F32),
        scratch_shapes=[pltpu.VMEM((NSA_H, t, LANES), F32)] * 2,
        compiler_params=_cp(("arbitrary", "arbitrary")),
        name="flash_" + mode,
    )(*args)


def _gather_pages(cache_ref, pt_ref, seq, buf, sem, slot, npg, start, halves=False):
    for j in range(npg):
        rows = pl.ds(j * PAGE, PAGE)
        if halves:
            cps = [pltpu.make_async_copy(cache_ref.at[pt_ref[seq, j], :, pl.ds(hf * LANES, LANES)],
                                         buf.at[slot, hf, rows], sem.at[slot]) for hf in range(2)]
        else:
            cps = [pltpu.make_async_copy(cache_ref.at[pt_ref[seq, j]], buf.at[slot, rows], sem.at[slot])]
        for cp in cps:
            if start:
                cp.start()
            else:
                cp.wait()


def _gather_pages_t(cache_ref, pt_ref, seq, buf, sem, slot, npg, start):
    for j in range(npg):
        cp = pltpu.make_async_copy(cache_ref.at[pt_ref[seq, j]], buf.at[slot, j], sem.at[slot])
        if start:
            cp.start()
        else:
            cp.wait()


def _compress_kernel(pt_ref, cache_ref, w1_ref, pe_ref, b1_ref, w2_ref, gk_ref, kc4_ref, vc4_ref, *scratch,
                     npg, transposed):
    s = pl.program_id(0)
    ns = pl.num_programs(0)
    slot = s % 2
    L = npg * PAGE
    n_c = L // CMP_STRIDE
    if transposed:
        buft, rb, first_scr, second_scr, sem = scratch
        gather = functools.partial(_gather_pages_t, cache_ref, pt_ref, buf=buft, sem=sem, npg=npg)
    else:
        buf, first_scr, second_scr, sem = scratch
        gather = functools.partial(_gather_pages, cache_ref, pt_ref, buf=buf, sem=sem, npg=npg, halves=True)

    @pl.when(s == 0)
    def _():
        gather(seq=0, slot=0, start=True)

    @pl.when(s + 1 < ns)
    def _():
        gather(seq=s + 1, slot=1 - slot, start=True)

    gather(seq=s, slot=slot, start=False)
    if transposed:
        def to_rows(j, c):
            for hf in range(2):
                rb[hf, pl.ds(pl.multiple_of(j * PAGE, PAGE), PAGE), :] = buft[slot, j, hf * LANES:(hf + 1) * LANES, :].T
            return c
        lax.fori_loop(0, npg, to_rows, 0, unroll=4 if npg % 4 == 0 else 1)
        rows_of = lambda hf, ds: rb[hf, ds, :]
    else:
        rows_of = lambda hf, ds: buf[slot, hf, ds, :]
    zeros16 = jnp.zeros((CPAD, 4 * LANES), BF16)
    kc4_ref[0, 0:CPAD, :] = zeros16
    kc4_ref[0, CPAD + n_c:2 * CPAD + n_c, :] = zeros16
    vc4_ref[0, 0:CPAD, :] = zeros16
    vc4_ref[0, CPAD + n_c:2 * CPAD + n_c, :] = zeros16

    rc = min(n_c, 256)
    npos = 4
    half = CMP_STRIDE * HD
    for c0 in range(0, n_c, rc):
        acc = [jnp.zeros((rc, 2 * CMP_HID), F32) for _ in range(4)]
        for p0 in range(0, CMP_STRIDE, npos):
            xs = [[rows_of(hf, pl.ds(c0 * CMP_STRIDE + p0 + i, rc, stride=CMP_STRIDE)) for i in range(npos)]
                  for hf in range(2)]
            for sg in range(4):
                lhs = jnp.concatenate([x[:, (sg % 2) * HD:(sg % 2 + 1) * HD] for x in xs[sg // 2]],
                                      axis=1).astype(BF16)
                acc[sg] = acc[sg] + _dot(lhs, w1_ref[sg // 2, p0 * HD:(p0 + npos) * HD, :])
        for sg in range(4):
            first_scr[sg, c0:c0 + rc, :] = acc[sg][:, 0:CMP_HID]
            second_scr[sg, c0:c0 + rc, :] = acc[sg][:, CMP_HID:2 * CMP_HID]
    second_scr[:, n_c:n_c + 8, :] = jnp.zeros((4, 8, CMP_HID), F32)
    for c0 in range(0, n_c, rc):
        outs = []
        for sg in range(4):
            sidx = sg // 2
            b = (_dot(pe_ref[sidx][:, 0:half], w1_ref[sidx])[0:1, 0:CMP_HID]
                 + _dot(pe_ref[sidx][:, half:2 * half], w1_ref[sidx])[0:1, CMP_HID:2 * CMP_HID] + b1_ref[sidx][0:1, :])
            hcur = first_scr[sg, c0:c0 + rc, :] + second_scr[sg, pl.ds(c0 + 1, rc), :] + b
            hcur = 0.5 * hcur * (1.0 + jnp.tanh(0.7978845608028654 * (hcur + 0.044715 * hcur * hcur * hcur)))
            outs.append(_dot(hcur.astype(BF16), w2_ref[sidx]))
        z = jnp.zeros((rc, HD), F32)
        kn = [o * lax.rsqrt(jnp.mean(o * o, axis=-1, keepdims=True) + EPS) * gk_ref[...] for o in outs[0:2]]
        rows = slice(CPAD + c0, CPAD + c0 + rc)
        for i, (g, p) in enumerate(((0, 0), (0, 1), (1, 0), (1, 1))):
            kk = jnp.concatenate([kn[g], z] if p == 0 else [z, kn[g]], axis=1)
            vv = jnp.concatenate([outs[2 + g], z] if p == 0 else [z, outs[2 + g]], axis=1)
            kc4_ref[0, rows, i * LANES:(i + 1) * LANES] = kk.astype(BF16)
            vc4_ref[0, rows, i * LANES:(i + 1) * LANES] = vv.astype(BF16)


def _compress(pt, cache, lw, npg, transposed):
    n_seq = pt.shape[0]
    L = npg * PAGE
    n_c = L // CMP_STRIDE
    rows = n_c + 2 * CPAD
    full = lambda a: pl.BlockSpec(a.shape, lambda s, pt_: (0,) * a.ndim)
    ws = [lw["cmp_w1"], lw["cmp_pe"], lw["cmp_b1"], lw["cmp_w2"], lw["cmp_gk"]]
    if transposed:
        scratch = [pltpu.VMEM((2, npg, 2 * LANES, PAGE), F32), pltpu.VMEM((2, L, LANES), F32)]
    else:
        scratch = [pltpu.VMEM((2, 2, L, LANES), F32)]
    scratch += [pltpu.VMEM((4, n_c + 8, CMP_HID), F32)] * 2
    return pl.pallas_call(
        functools.partial(_compress_kernel, npg=npg, transposed=transposed),
        grid_spec=pltpu.PrefetchScalarGridSpec(
            num_scalar_prefetch=1,
            grid=(n_seq,),
            in_specs=[pl.BlockSpec(memory_space=pl.ANY)] + [full(a) for a in ws],
            out_specs=[pl.BlockSpec((1, rows, 4 * LANES), lambda s, pt_: (s, 0, 0))] * 2,
            scratch_shapes=scratch + [pltpu.SemaphoreType.DMA((2,))]),
        out_shape=[jax.ShapeDtypeStruct((n_seq, rows, 4 * LANES), BF16)] * 2,
        compiler_params=_cp(("arbitrary",)),
        name="compress",
    )(pt, cache, *ws)


def _cmp_topk_kernel(q_ref, kc4_ref, vc4_ref, mm_ref, dcn_ref, oc_ref, sel_ref, *, tq, qpos0, n_c, n_sel, topk):
    i = pl.program_id(1)
    q0 = qpos0 + i * tq
    c0p = pl.multiple_of((q0 // TQ) * CPAD, CPAD)
    nsp = mm_ref.shape[1]
    qpos = q0 + _row((tq, 1))
    wn = 2 * CPAD
    colf = _lane((tq, n_c))
    valid_far = colf < (c0p - CPAD)
    c_near = c0p - CPAD + _lane((tq, wn))
    valid_near = (c_near >= 0) & (c_near * CMP_STRIDE + (CMP_BLOCK - 1) <= qpos)
    far_rows = slice(CPAD, CPAD + n_c)
    near_rows = pl.ds(c0p, wn)
    res = [None] * NSA_H
    for g in range(NSA_G):
        ps_far = jnp.zeros((tq, n_c), F32)
        ps_near = jnp.zeros((tq, wn), F32)
        for hp in range(NSA_HPG):
            h = g * NSA_HPG + hp
            gp = (g * 2 + h % 2) * LANES
            qh = q_ref[0, :, (h // 2) * LANES:(h // 2 + 1) * LANES]
            s_far = jnp.where(valid_far, _dot_nt(qh, kc4_ref[0, far_rows, gp:gp + LANES]), NEG)
            s_near = _dot_nt(qh, kc4_ref[0, near_rows, gp:gp + LANES]) + dcn_ref[h][:, 0:wn]
            s_near = jnp.where(valid_near, s_near, NEG)
            m = jnp.maximum(jnp.maximum(jnp.max(s_far, axis=1, keepdims=True), jnp.max(s_near, axis=1, keepdims=True)),
                            M_INIT)
            pf = jnp.exp(s_far - m)
            pn = jnp.exp(s_near - m)
            inv = 1.0 / jnp.maximum(jnp.sum(pf, axis=1, keepdims=True) + jnp.sum(pn, axis=1, keepdims=True), 1e-30)
            pf = pf * inv
            pn = pn * inv
            res[h] = (_dot(pf.astype(BF16), vc4_ref[0, far_rows, gp:gp + LANES])
                      + _dot(pn.astype(BF16), vc4_ref[0, near_rows, gp:gp + LANES]))
            ps_far = ps_far + pf
            ps_near = ps_near + pn
        mf = mm_ref[far_rows, :]
        mn = mm_ref[near_rows, :]
        imp = (_dot(_split_bf16(ps_far), jnp.concatenate([mf, mf], axis=0))
               + _dot(_split_bf16(ps_near), jnp.concatenate([mn, mn], axis=0)))
        jl = _lane((tq, nsp))
        cur = qpos // SEL_BLOCK
        forced = (jl == 0) | (jl == cur) | (jl == cur - 1)
        valid = jl * SEL_BLOCK <= qpos
        imp = jnp.where(valid, imp + jnp.where(forced, FORCE_BONUS, 0.0), NEG)
        imp = jnp.where(jl < n_sel, imp, PAD_IMP)
        if topk:
            sel_ref[0, :, g * nsp:(g + 1) * nsp] = _topk_mask_t(imp.T).T.astype(BF16)
        else:
            sel_ref[0, :, g * nsp:(g + 1) * nsp] = imp
    for pp in range(NSA_H // 2):
        oc_ref[0, :, pp * LANES:(pp + 1) * LANES] = res[2 * pp] + res[2 * pp + 1]


def _topk_mask_t(imp_t):
    rowf = _row(imp_t.shape).astype(F32)
    sel = jnp.zeros(imp_t.shape, jnp.bool_)
    for _ in range(TOP_N):
        mx = jnp.max(imp_t, axis=0, keepdims=True)
        am = jnp.min(jnp.where(imp_t == mx, rowf, 1e9), axis=0, keepdims=True)
        hit = rowf == am
        sel = sel | hit
        imp_t = jnp.where(hit, -jnp.inf, imp_t)
    return jnp.where(sel, 0.0, NEG)


def _topk_rounds(imp, g, idx_acc):
    jf = _lane(imp.shape).astype(F32)
    lane_i = _lane(idx_acc.shape)
    sel = jnp.zeros(imp.shape, jnp.bool_)
    for kk in range(TOP_N):
        mx = jnp.max(imp, axis=1, keepdims=True)
        am = jnp.min(jnp.where(imp == mx, jf, 1e9), axis=1, keepdims=True)
        hit = jf == am
        sel = sel | hit
        imp = jnp.where(hit, -jnp.inf, imp)
        idx_acc = jnp.where(lane_i == g * TOP_N + kk, am.astype(jnp.int32), idx_acc)
    return sel, idx_acc


def _cmp_topk(qb, kc4, vc4, dcn, tq, qpos0, n_sel, topk):
    n_seq, T, _ = qb.shape
    rows = kc4.shape[1]
    n_c = rows - 2 * CPAD
    nsp = -(-n_sel // LANES) * LANES
    mm = _cmp_to_sel(n_c, n_sel, nsp)
    whole = lambda a: pl.BlockSpec((1,) + a.shape[1:], lambda n, i: (n, 0, 0))
    out_specs = [pl.BlockSpec((1, tq, NSA_H * HD), lambda n, i: (n, i, 0)),
                 pl.BlockSpec((1, tq, NSA_G * nsp), lambda n, i: (n, i, 0))]
    out_shape = [jax.ShapeDtypeStruct((n_seq, T, NSA_H * HD), F32),
                 jax.ShapeDtypeStruct((n_seq, T, NSA_G * nsp), BF16 if topk else F32)]
    return pl.pallas_call(
        functools.partial(_cmp_topk_kernel, tq=tq, qpos0=qpos0, n_c=n_c, n_sel=n_sel, topk=topk),
        grid=(n_seq, T // tq),
        in_specs=[pl.BlockSpec((1, tq, NSA_H * HD), lambda n, i: (n, i, 0)), whole(kc4), whole(vc4),
                  pl.BlockSpec(mm.shape, lambda n, i: (0, 0)),
                  pl.BlockSpec(dcn.shape, lambda n, i: (0, 0, 0))],
        out_specs=out_specs,
        out_shape=out_shape,
        compiler_params=_cp(("arbitrary", "arbitrary")),
        name="cmp_topk" if topk else "cmp_attn",
    )(qb, kc4, vc4, mm, dcn)


def _topk_kernel(imp_ref, idx_ref, *, nsp):
    idx_acc = jnp.zeros(idx_ref.shape, jnp.int32)
    for g in range(NSA_G):
        _, idx_acc = _topk_rounds(imp_ref[:, g * nsp:(g + 1) * nsp], g, idx_acc)
    idx_ref[...] = idx_acc


def _topk(imp, nsp):
    R = imp.shape[0]
    return pl.pallas_call(
        functools.partial(_topk_kernel, nsp=nsp),
        grid=(1,),
        in_specs=[pl.BlockSpec(imp.shape, lambda i: (0, 0))],
        out_specs=pl.BlockSpec((R, LANES), lambda i: (0, 0)),
        out_shape=jax.ShapeDtypeStruct((R, LANES), jnp.int32),
        compiler_params=_cp(("arbitrary",)),
        name="topk",
    )(imp)


def _mem_kv_kernel(mem_ref, g_ref, w_ref, s2_ref, gk_ref, o_ref):
    x = mem_ref[0]
    xn = (x * lax.rsqrt(jnp.mean(x * x, axis=-1, keepdims=True) + EPS) * g_ref[...]).astype(BF16)
    kv = _dot(xn, w_ref[...])
    nk = MEM_H * HD
    for v in range(nk // LANES):
        y = kv[:, v * LANES:(v + 1) * LANES]
        o_ref[0, :, v * LANES:(v + 1) * LANES] = y * lax.rsqrt(_seg_ms(y, s2_ref[0]) + EPS) * gk_ref[...]
    o_ref[0, :, nk:2 * nk] = kv[:, nk:2 * nk]


def _mem_kv(mem, lw):
    N, M, _ = mem.shape
    full = lambda a: pl.BlockSpec(a.shape, lambda n: (0,) * a.ndim)
    ws = [lw["mem_norm_g"], lw["w_mem_kv"], lw["s2"], lw["mem_gk"]]
    return pl.pallas_call(
        _mem_kv_kernel,
        grid=(N,),
        in_specs=[pl.BlockSpec((1, M, D_MODEL), lambda n: (n, 0, 0))] + [full(a) for a in ws],
        out_specs=pl.BlockSpec((1, M, 2 * MEM_H * HD), lambda n: (n, 0, 0)),
        out_shape=jax.ShapeDtypeStruct((N, M, 2 * MEM_H * HD), F32),
        compiler_params=_cp(("arbitrary",)),
        name="mem_kv",
    )(mem, *ws)


def _mem_attn_kernel(q_ref, kv_ref, o_ref):
    nk = MEM_H * HD
    tm = q_ref.shape[1]
    lo = _lane((kv_ref.shape[1], LANES)) < HD
    lo_o = _lane((tm, LANES)) < HD
    for pp in range(MEM_H // 2):
        qp = q_ref[0, :, pp * LANES:(pp + 1) * LANES]
        kp = kv_ref[0, :, pp * LANES:(pp + 1) * LANES]
        vp = kv_ref[0, :, nk + pp * LANES:nk + (pp + 1) * LANES].astype(BF16)
        r = []
        for par in range(2):
            kz = jnp.where(lo if par == 0 else ~lo, kp, 0.0).astype(BF16)
            s = _dot_nt(qp, kz)
            p = jnp.exp(s - jnp.max(s, axis=1, keepdims=True))
            r.append(_dot(p.astype(BF16), vp) / jnp.sum(p, axis=1, keepdims=True))
        o_ref[0, :, pp * LANES:(pp + 1) * LANES] = jnp.where(lo_o, r[0], r[1])


def _mem_attn(qm, mkv, tm):
    N, T, w = qm.shape
    return pl.pallas_call(
        _mem_attn_kernel,
        grid=(N, T // tm),
        in_specs=[pl.BlockSpec((1, tm, w), lambda n, i: (n, i, 0)),
                  pl.BlockSpec((1,) + mkv.shape[1:], lambda n, i: (n, 0, 0))],
        out_specs=pl.BlockSpec((1, tm, w), lambda n, i: (n, i, 0)),
        out_shape=jax.ShapeDtypeStruct((N, T, w), F32),
        compiler_params=_cp(("arbitrary", "arbitrary")),
        name="mem_attn",
    )(qm, mkv)


def _mla_decode_kernel(pt_ref, cache_ref, q6_ref, qr_ref, own_ref, wuk_ref, wukt_ref, wuv_ref, gk_ref, o_ref,
                       buf, sem, *, npg, pc):
    s = pl.program_id(0)
    ns = pl.num_programs(0)
    slot = s % 2
    w = MLA_H * HD

    @pl.when(s == 0)
    def _():
        _gather_pages_t(cache_ref, pt_ref, 0, buf, sem, 0, npg, True)

    @pl.when(s + 1 < ns)
    def _():
        _gather_pages_t(cache_ref, pt_ref, s + 1, buf, sem, 1 - slot, npg, True)

    _gather_pages_t(cache_ref, pt_ref, s, buf, sem, slot, npg, False)
    q6g = q6_ref[0].astype(F32) * gk_ref[...]
    qabs = _dot_nt(q6g.astype(BF16), wuk_ref[...]).astype(BF16)
    qr = qr_ref[0][:, 0:MLA_ROPE]
    rowi = _row((8, pc * PAGE))

    def body(i, carry):
        m_prev, l_prev, acc = carry
        ct = jnp.concatenate([buf[slot, i * pc + j, 0:MLA_KVL, :] for j in range(pc)], axis=1).astype(BF16)
        krt = jnp.concatenate([buf[slot, i * pc + j, MLA_KVL:MLA_KVL + MLA_ROPE, :] for j in range(pc)],
                              axis=1).astype(BF16)
        kraw = _dot(wukt_ref[...], ct)
        sq = kraw * kraw
        ms = jnp.zeros((8, pc * PAGE), F32)
        for h in range(MLA_H):
            ms = jnp.where(rowi == h, jnp.sum(sq[h * HD:(h + 1) * HD], axis=0, keepdims=True), ms)
        sc = _dot(qabs, ct) * lax.rsqrt(ms * (1.0 / HD) + EPS) + _dot(qr, krt)
        m_new = jnp.maximum(m_prev, jnp.max(sc, axis=1, keepdims=True))
        alpha = jnp.exp(m_prev - m_new)
        p = jnp.exp(sc - m_new)
        return (m_new, alpha * l_prev + jnp.sum(p, axis=1, keepdims=True),
                alpha * acc + _dot_nt(p.astype(BF16), ct))

    carry = (jnp.full((8, 1), M_INIT, F32), jnp.zeros((8, 1), F32), jnp.zeros((8, MLA_KVL), F32))
    m_prev, l_prev, acc = lax.fori_loop(0, npg // pc, body, carry)
    own = own_ref[0]
    c_own = own[0:1, 0:MLA_KVL]
    kraw_o = _dot(own[:, 0:MLA_KVL].astype(BF16), wuk_ref[...])[0:1, :]
    own_head = (_lane((8, w)) // HD) == _row((8, w))
    ms_o = jnp.sum(jnp.where(own_head, kraw_o * kraw_o, 0.0), axis=1, keepdims=True)
    s_own = (jnp.sum(q6g * kraw_o, axis=1, keepdims=True) * lax.rsqrt(ms_o * (1.0 / HD) + EPS)
             + jnp.sum(qr.astype(F32) * own[0:1, MLA_KVL:MLA_KVL + MLA_ROPE], axis=1, keepdims=True))
    m_new = jnp.maximum(m_prev, s_own)
    alpha = jnp.exp(m_prev - m_new)
    p_own = jnp.exp(s_own - m_new)
    l_fin = alpha * l_prev + p_own
    acc = alpha * acc + p_own * c_own
    o_lat = (acc / jnp.maximum(l_fin, 1e-30)).astype(BF16)
    res = _dot(o_lat, wuv_ref[...])
    o_ref[0] = jnp.broadcast_to(jnp.sum(jnp.where(own_head, res, 0.0), axis=0, keepdims=True), (8, w))


def _mla_decode(pt, cache, q6, qr, own, lw, npg):
    DB = pt.shape[0]
    pc = next(c for c in (16, 8, 4, 2, 1) if npg % c == 0)
    full = lambda a: pl.BlockSpec(a.shape, lambda s, pt_: (0,) * a.ndim)
    per = lambda a: pl.BlockSpec((1,) + a.shape[1:], lambda s, pt_: (s, 0, 0))
    ws = [lw["w_uk"], lw["w_uk_t"], lw["w_uv"], lw["mla_gk"]]
    w = MLA_H * HD
    return pl.pallas_call(
        functools.partial(_mla_decode_kernel, npg=npg, pc=pc),
        grid_spec=pltpu.PrefetchScalarGridSpec(
            num_scalar_prefetch=1,
            grid=(DB,),
            in_specs=[pl.BlockSpec(memory_space=pl.ANY), per(q6), per(qr), per(own)] + [full(a) for a in ws],
            out_specs=pl.BlockSpec((1, 8, w), lambda s, pt_: (s, 0, 0)),
            scratch_shapes=[pltpu.VMEM((2, npg, MLA_KVL + MLA_ROPE, PAGE), F32), pltpu.SemaphoreType.DMA((2,))]),
        out_shape=jax.ShapeDtypeStruct((DB, 8, w), F32),
        compiler_params=_cp(("arbitrary",)),
        name="mla_decode",
    )(pt, cache, q6, qr, own, *ws)


def _decode_core(q8, keys, bias, valid, own_row, own_bias, own_flag):
    kb = keys.astype(BF16)
    s = _dot_nt(q8, kb) + bias
    if valid is not None:
        s = jnp.where(valid, s, NEG)
    s_own = jnp.sum(q8.astype(F32) * own_row, axis=1, keepdims=True) + own_bias
    s_own = jnp.where(own_flag, s_own, NEG)
    m = jnp.maximum(jnp.maximum(jnp.max(s, axis=1, keepdims=True), s_own), M_INIT)
    p = jnp.exp(s - m)
    p_own = jnp.exp(s_own - m)
    l = jnp.sum(p, axis=1, keepdims=True) + p_own
    return (_dot(p.astype(BF16), kb) + p_own * own_row) / jnp.maximum(l, 1e-30)


def _place_heads(res):
    v0 = res[0][:, 2 * HD:4 * HD]
    v1 = res[1][:, 2 * HD:4 * HD]
    v0r = pltpu.roll(v0, HD, 1)
    v1r = pltpu.roll(v1, HD, 1)
    lo = _lane((1, LANES)) < HD
    row = jnp.concatenate([jnp.where(lo, v0[0:1], v0r[1:2]), jnp.where(lo, v0[2:3], v1[0:1]),
                           jnp.where(lo, v1r[1:2], v1[2:3])], axis=1)
    return jnp.broadcast_to(row, (8, NSA_H * HD))


def _slc_decode_kernel(idx_ref, pt_ref, cache_ref, q8_ref, own_ref, bt_ref, d0_ref, o_ref, buf, sem, *, n_past_blk):
    s = pl.program_id(0)
    ns = pl.num_programs(0)
    slot = s % 2
    nblk = NSA_G * TOP_N
    bpp = PAGE // SEL_BLOCK

    def copies(seq, sl, start):
        for j in range(nblk):
            b = jnp.minimum(idx_ref[seq, j], n_past_blk - 1)
            src = pt_ref[seq, b // bpp] * bpp + b % bpp
            cp = pltpu.make_async_copy(cache_ref.at[src], buf.at[sl, pl.ds(j * SEL_BLOCK, SEL_BLOCK)], sem.at[sl])
            if start:
                cp.start()
            else:
                cp.wait()

    @pl.when(s == 0)
    def _():
        copies(0, 0, True)

    @pl.when(s + 1 < ns)
    def _():
        copies(s + 1, 1 - slot, True)

    copies(s, slot, False)
    own = own_ref[0][0:1, :]
    lo = _lane((8, LANES)) < HD
    nk = TOP_N * SEL_BLOCK
    res = []
    for g in range(NSA_G):
        pieces, vpieces = [], []
        own_flag = idx_ref[s, g * TOP_N] == n_past_blk
        for jj in range(TOP_N // 2):
            ia = idx_ref[s, g * TOP_N + 2 * jj]
            ib = idx_ref[s, g * TOP_N + 2 * jj + 1]
            own_flag = own_flag | (ia == n_past_blk) | (ib == n_past_blk)
            z = jnp.zeros((8, LANES), F32)
            bias = (jnp.where(ia == n_past_blk - 1, bt_ref[g, 0], z) + jnp.where(ia == n_past_blk - 2, bt_ref[g, 2], z)
                    + jnp.where(ib == n_past_blk - 1, bt_ref[g, 1], z) + jnp.where(ib == n_past_blk - 2, bt_ref[g, 3], z))
            pieces.append(bias)
            va = jnp.where(ia < n_past_blk, 1, 0)
            vb = jnp.where(ib < n_past_blk, 1, 0)
            vpieces.append(jnp.where(lo, va, vb) > 0)
        keys = buf[slot, g * nk:(g + 1) * nk, :]
        res.append(_decode_core(q8_ref[0, g], keys, jnp.concatenate(pieces, axis=1),
                                jnp.concatenate(vpieces, axis=1), own, d0_ref[g][:, 0:1], own_flag))
    o_ref[0] = _place_heads(res)


def _slc_decode(idx, pt, cache, q8, own, bt, d0, n_past_blk):
    DB = idx.shape[0]
    per3 = lambda a: pl.BlockSpec((1,) + a.shape[1:], lambda s, i_, p_: (s,) + (0,) * (a.ndim - 1))
    full = lambda a: pl.BlockSpec(a.shape, lambda s, i_, p_: (0,) * a.ndim)
    return pl.pallas_call(
        functools.partial(_slc_decode_kernel, n_past_blk=n_past_blk),
        grid_spec=pltpu.PrefetchScalarGridSpec(
            num_scalar_prefetch=2,
            grid=(DB,),
            in_specs=[pl.BlockSpec(memory_space=pl.ANY), per3(q8), per3(own), full(bt), full(d0)],
            out_specs=pl.BlockSpec((1, 8, NSA_H * HD), lambda s, i_, p_: (s, 0, 0)),
            scratch_shapes=[pltpu.VMEM((2, NSA_G * TOP_N * SEL_BLOCK, 2 * LANES), F32), pltpu.SemaphoreType.DMA((2,))]),
        out_shape=jax.ShapeDtypeStruct((DB, 8, NSA_H * HD), F32),
        compiler_params=_cp(("arbitrary",)),
        name="slc_decode",
    )(idx, pt, cache, q8, own, bt, d0)


def _win_decode_kernel(wb_ref, q8_ref, own_ref, bw_ref, d0_ref, o_ref, *, first_valid):
    keys = wb_ref[0]
    n = keys.shape[0]
    own = own_ref[0][0:1, :]
    valid = _lane((8, n)) >= first_valid
    o_ref[0] = _place_heads([_decode_core(q8_ref[0, g], keys, bw_ref[g], valid, own, d0_ref[g][:, 0:1], True)
                             for g in range(NSA_G)])


def _win_decode(wbuf, q8, own, bw, d0, first_valid):
    DB, wb, _ = wbuf.shape
    per = lambda a: pl.BlockSpec((1,) + a.shape[1:], lambda s: (s,) + (0,) * (a.ndim - 1))
    full = lambda a: pl.BlockSpec(a.shape, lambda s: (0,) * a.ndim)
    return pl.pallas_call(
        functools.partial(_win_decode_kernel, first_valid=first_valid),
        grid=(DB,),
        in_specs=[per(wbuf), per(q8), per(own), full(bw), full(d0)],
        out_specs=pl.BlockSpec((1, 8, NSA_H * HD), lambda s: (s, 0, 0)),
        out_shape=jax.ShapeDtypeStruct((DB, 8, NSA_H * HD), F32),
        compiler_params=_cp(("arbitrary",)),
        name="win_decode",
    )(wbuf, q8, own, bw, d0)


def _mem_decode_kernel(kv_ref, q8_ref, o_ref):
    kb = kv_ref[0].astype(BF16)
    s = _dot_nt(q8_ref[0], kb)
    p = jnp.exp(s - jnp.max(s, axis=1, keepdims=True))
    nk = MEM_H * HD
    res = (_dot(p.astype(BF16), kb) / jnp.sum(p, axis=1, keepdims=True))[:, nk:2 * nk]
    own_head = (_lane((8, nk)) // HD) == _row((8, nk))
    o_ref[0] = jnp.broadcast_to(jnp.sum(jnp.where(own_head, res, 0.0), axis=0, keepdims=True), (8, nk))


def _mem_decode(mkv, q8):
    DB, M, w = mkv.shape
    return pl.pallas_call(
        _mem_decode_kernel,
        grid=(DB,),
        in_specs=[pl.BlockSpec((1, M, w), lambda s: (s, 0, 0)), pl.BlockSpec((1, 8, w), lambda s: (s, 0, 0))],
        out_specs=pl.BlockSpec((1, 8, w // 2), lambda s: (s, 0, 0)),
        out_shape=jax.ShapeDtypeStruct((DB, 8, w // 2), F32),
        compiler_params=_cp(("arbitrary",)),
        name="mem_decode",
    )(mkv, q8)


def _decode_group(qg, kt, vt, bias, valid, k_own, v_own, own_bias, own_flag):
    s = _dot(qg.astype(BF16), kt.astype(BF16)) + bias
    s = jnp.where(valid, s, NEG)
    s_own = jnp.sum(qg * k_own, axis=1, keepdims=True) + own_bias
    s_own = jnp.where(own_flag, s_own, NEG)
    m = jnp.maximum(jnp.maximum(jnp.max(s, axis=1, keepdims=True), s_own), M_INIT)
    p = jnp.exp(s - m)
    p_own = jnp.exp(s_own - m)
    l = jnp.sum(p, axis=1, keepdims=True) + p_own
    return (_dot_nt(p.astype(BF16), vt.astype(BF16)) + p_own * v_own) / jnp.maximum(l, 1e-30)


def _head_row(res):
    d0 = jnp.concatenate([res[0], res[0]], axis=1)
    d1 = jnp.concatenate([res[1], res[1]], axis=1)
    lo = _lane((1, LANES)) < HD
    row = jnp.concatenate([jnp.where(lo, d0[0:1], d0[1:2]), jnp.where(lo, d0[2:3], d1[0:1]),
                           jnp.where(lo, d1[1:2], d1[2:3])], axis=1)
    return jnp.broadcast_to(row, (8, NSA_H * HD))


def _slc_decode_t_kernel(idx_ref, pt_ref, cache_ref, qg_ref, own_ref, bt_ref, d0_ref, o_ref, buf, sem, *, n_past_blk):
    s = pl.program_id(0)
    ns = pl.num_programs(0)
    slot = s % 2
    nblk = NSA_G * TOP_N
    bpp = PAGE // SEL_BLOCK

    def copies(seq, sl, start):
        for j in range(nblk):
            b = jnp.minimum(idx_ref[seq, j], n_past_blk - 1)
            cp = pltpu.make_async_copy(cache_ref.at[pt_ref[seq, b // bpp]], buf.at[sl, j], sem.at[sl])
            if start:
                cp.start()
            else:
                cp.wait()

    @pl.when(s == 0)
    def _():
        copies(0, 0, True)

    @pl.when(s + 1 < ns)
    def _():
        copies(s + 1, 1 - slot, True)

    copies(s, slot, False)
    own = own_ref[0][0:1, :]
    lane = _lane((8, PAGE))
    res = []
    for g in range(NSA_G):
        pieces, vpieces = [], []
        own_flag = idx_ref[s, g * TOP_N] == n_past_blk
        for j in range(TOP_N):
            ib = idx_ref[s, g * TOP_N + j]
            own_flag = own_flag | (ib == n_past_blk)
            z = jnp.zeros((8, PAGE), F32)
            pieces.append(jnp.where(ib == n_past_blk - 1, bt_ref[g, 0], z) + jnp.where(ib == n_past_blk - 2, bt_ref[g, 1], z))
            half = jnp.where(ib < n_past_blk, ib % bpp, bpp)
            vpieces.append((lane // SEL_BLOCK) == half)
        kt = jnp.concatenate([buf[slot, g * TOP_N + j, g * HD:(g + 1) * HD, :] for j in range(TOP_N)], axis=1)
        vt = jnp.concatenate([buf[slot, g * TOP_N + j, (NSA_G + g) * HD:(NSA_G + g + 1) * HD, :] for j in range(TOP_N)],
                             axis=1)
        res.append(_decode_group(qg_ref[0, g * 8:(g + 1) * 8, :], kt, vt, jnp.concatenate(pieces, axis=1),
                                 jnp.concatenate(vpieces, axis=1), own[:, g * HD:(g + 1) * HD],
                                 own[:, (NSA_G + g) * HD:(NSA_G + g + 1) * HD], d0_ref[g][:, 0:1], own_flag))
    o_ref[0] = _head_row(res)


def _slc_decode_t(idx, pt, cache, qg, own, bt, d0, n_past_blk):
    DB = idx.shape[0]
    per3 = lambda a: pl.BlockSpec((1,) + a.shape[1:], lambda s, i_, p_: (s,) + (0,) * (a.ndim - 1))
    full = lambda a: pl.BlockSpec(a.shape, lambda s, i_, p_: (0,) * a.ndim)
    return pl.pallas_call(
        functools.partial(_slc_decode_t_kernel, n_past_blk=n_past_blk),
        grid_spec=pltpu.PrefetchScalarGridSpec(
            num_scalar_prefetch=2,
            grid=(DB,),
            in_specs=[pl.BlockSpec(memory_space=pl.ANY), per3(qg), per3(own), full(bt), full(d0)],
            out_specs=pl.BlockSpec((1, 8, NSA_H * HD), lambda s, i_, p_: (s, 0, 0)),
            scratch_shapes=[pltpu.VMEM((2, NSA_G * TOP_N, 2 * LANES, PAGE), F32), pltpu.SemaphoreType.DMA((2,))]),
        out_shape=jax.ShapeDtypeStruct((DB, 8, NSA_H * HD), F32),
        compiler_params=_cp(("arbitrary",)),
        name="slc_decode",
    )(idx, pt, cache, qg, own, bt, d0)


def _win_decode_t_kernel(wt_ref, qg_ref, own_ref, bw_ref, d0_ref, o_ref, *, first_valid):
    n = wt_ref.shape[2]
    own = own_ref[0][0:1, :]
    valid = _lane((8, n)) >= first_valid
    res = [_decode_group(qg_ref[0, g * 8:(g + 1) * 8, :], wt_ref[0, g * HD:(g + 1) * HD, :],
                         wt_ref[0, (NSA_G + g) * HD:(NSA_G + g + 1) * HD, :], bw_ref[g], valid,
                         own[:, g * HD:(g + 1) * HD], own[:, (NSA_G + g) * HD:(NSA_G + g + 1) * HD],
                         d0_ref[g][:, 0:1], True) for g in range(NSA_G)]
    o_ref[0] = _head_row(res)


def _win_decode_t(wt, qg, own, bw, d0, first_valid):
    DB = wt.shape[0]
    per = lambda a: pl.BlockSpec((1,) + a.shape[1:], lambda s: (s,) + (0,) * (a.ndim - 1))
    full = lambda a: pl.BlockSpec(a.shape, lambda s: (0,) * a.ndim)
    return pl.pallas_call(
        functools.partial(_win_decode_t_kernel, first_valid=first_valid),
        grid=(DB,),
        in_specs=[per(wt), per(qg), per(own), full(bw), full(d0)],
        out_specs=pl.BlockSpec((1, 8, NSA_H * HD), lambda s: (s, 0, 0)),
        out_shape=jax.ShapeDtypeStruct((DB, 8, NSA_H * HD), F32),
        compiler_params=_cp(("arbitrary",)),
        name="win_decode",
    )(wt, qg, own, bw, d0)


def _mem_decode_t_kernel(kvt_ref, q8_ref, o_ref):
    nk = MEM_H * HD
    s = _dot(q8_ref[0], kvt_ref[0, 0:nk, :].astype(BF16))
    p = jnp.exp(s - jnp.max(s, axis=1, keepdims=True))
    res = _dot_nt(p.astype(BF16), kvt_ref[0, nk:2 * nk, :].astype(BF16)) / jnp.sum(p, axis=1, keepdims=True)
    own_head = (_lane((8, nk)) // HD) == _row((8, nk))
    o_ref[0] = jnp.broadcast_to(jnp.sum(jnp.where(own_head, res, 0.0), axis=0, keepdims=True), (8, nk))


def _mem_decode_t(kvt, q8):
    DB, w, M = kvt.shape
    return pl.pallas_call(
        _mem_decode_t_kernel,
        grid=(DB,),
        in_specs=[pl.BlockSpec((1, w, M), lambda s: (s, 0, 0)), pl.BlockSpec((1, 8, w // 2), lambda s: (s, 0, 0))],
        out_specs=pl.BlockSpec((1, 8, w // 2), lambda s: (s, 0, 0)),
        out_shape=jax.ShapeDtypeStruct((DB, 8, w // 2), F32),
        compiler_params=_cp(("arbitrary",)),
        name="mem_decode",
    )(kvt, q8)


def _merge_kernel(x_ref, oa_ref, oc_ref, os_ref, ow_ref, gb_ref, om_ref, gates_ref, e2_ref, wout_ref, y_ref):
    w = NSA_H * HD
    ge = _dot(_split_bf16(gb_ref[...]), e2_ref[...])
    ob = oc_ref[...] * ge[:, 0:w] + os_ref[...] * ge[:, w:2 * w] + ow_ref[...] * ge[:, 2 * w:3 * w]
    z = jnp.concatenate([oa_ref[...], ob, om_ref[...]], axis=1) * gates_ref[...]
    y_ref[...] = x_ref[...] + _dot(z.astype(BF16), wout_ref[...])


def _merge(x2, oa, oc, osl, ow, gb, om, gates, lw, tm):
    R = x2.shape[0]
    row = lambda a: pl.BlockSpec((tm, a.shape[1]), lambda i: (i, 0))
    full = lambda a: pl.BlockSpec(a.shape, lambda i: (0,) * a.ndim)
    acts = [x2, oa, oc, osl, ow, gb, om, gates]
    return pl.pallas_call(
        _merge_kernel,
        grid=(R // tm,),
        in_specs=[row(a) for a in acts] + [full(lw["e2"]), full(lw["w_out"])],
        out_specs=pl.BlockSpec((tm, D_MODEL), lambda i: (i, 0)),
        out_shape=jax.ShapeDtypeStruct((R, D_MODEL), F32),
        compiler_params=_cp(("arbitrary",)),
        name="merge",
    )(*acts, lw["e2"], lw["w_out"])


def _layer_weights(l, norm_g, w_in, mla_q_norm, mla_w_uq, mla_kv_norm, mla_w_ukv, mla_nope_g, mla_rope_g,
                   nsa_qk_g, nsa_cmp_pe, nsa_cmp_w1, nsa_cmp_b1, nsa_cmp_w2, mem_norm_g, w_mem_kv, mem_qk_g, w_out):
    w = w_in[l]
    o = np.cumsum((0, 256, 128, 32, 384, 384, 768, 18, 384, 256, 256))
    z = lambda n: jnp.zeros((D_MODEL, n), F32)
    w_in_p = jnp.concatenate([
        w[:, o[0]:o[1]], w[:, o[1]:o[2]],
        z(MLA_NOPE), w[:, o[2]:o[3]], z(LANES - MLA_NOPE - MLA_ROPE),
        w[:, o[4]:o[5]], w[:, o[5]:o[6]],
        w[:, o[6]:o[7]], z(LANES - 18),
        w[:, o[8]:o[9]],
        w[:, o[3]:o[4]], w[:, o[7]:o[8]], w[:, o[9]:o[10]]], axis=1).astype(BF16)
    uq = mla_w_uq[l].reshape(MLA_QL, MLA_H, MLA_QK)
    w_uq = jnp.pad(uq, ((0, 0), (0, 0), (0, LANES - MLA_QK))).reshape(MLA_QL, MLA_H * LANES).astype(BF16)
    ukv = mla_w_ukv[l].reshape(MLA_KVL, MLA_H, 2 * HD)
    w_uk = ukv[:, :, :HD].reshape(MLA_KVL, MLA_H * HD).astype(BF16)
    w_uk_p = jnp.pad(ukv[:, :, :HD], ((0, 0), (0, 0), (0, HD))).reshape(MLA_KVL, MLA_H * LANES).astype(BF16)
    w_uv = ukv[:, :, HD:].reshape(MLA_KVL, MLA_H * HD).astype(BF16)
    uv2 = ukv[:, :, HD:].reshape(MLA_KVL, MLA_H // 2, 2, HD)
    zv = jnp.zeros_like(uv2[:, :, 0])
    w_uv_p = jnp.stack([uv2[:, :, 0], zv, zv, uv2[:, :, 1]], axis=2).reshape(MLA_KVL, MLA_H * LANES).astype(BF16)
    sa = MLA_QK ** -0.5
    sb = HD ** -0.5
    pad = lambda v: jnp.pad(v, (0, D_MODEL - v.shape[0]))
    z32 = jnp.zeros((32,), F32)
    z64 = jnp.zeros((64,), F32)
    qk = nsa_qk_g[l]
    gv = jnp.stack([
        norm_g[l], pad(mla_q_norm[l]), pad(mla_kv_norm[l]),
        pad(jnp.concatenate([mla_nope_g[l, 0] * sa, mla_rope_g[l, 0] * sa, z32])),
        pad(jnp.concatenate([z64, mla_rope_g[l, 1], z32])),
        pad(jnp.concatenate([mla_nope_g[l, 1], z64])),
        pad(jnp.tile(qk[0] * sb, 2)), pad(jnp.tile(qk[2], 2)), pad(jnp.tile(qk[3], 2)),
        pad(jnp.tile(mem_qk_g[l, 0] * sb, 2))] + [jnp.zeros((D_MODEL,), F32)] * 6)
    pe = nsa_cmp_pe[l].reshape(2, 1, CMP_BLOCK * HD)
    return dict(
        gv=gv, w_in=w_in_p, w_uq=w_uq, w_uk=w_uk, w_uk_t=w_uk.T, w_uk_p=w_uk_p, w_uv=w_uv, w_uv_p=w_uv_p, s2=_seg_mats(),
        mla_gk=jnp.tile(mla_nope_g[l, 1], MLA_H)[None, :],
        cmp_w1=jnp.concatenate([nsa_cmp_w1[l][:, :CMP_STRIDE * HD], nsa_cmp_w1[l][:, CMP_STRIDE * HD:]],
                               axis=-1).astype(BF16), cmp_pe=jnp.broadcast_to(pe, (2, 8, CMP_BLOCK * HD)).astype(BF16),
        cmp_b1=jnp.broadcast_to(nsa_cmp_b1[l][:, None, :], (2, 8, CMP_HID)),
        cmp_w2=nsa_cmp_w2[l].astype(BF16), cmp_gk=qk[1][None, :],
        mem_norm_g=mem_norm_g[l][None, :], w_mem_kv=w_mem_kv[l].astype(BF16),
        mem_gk=jnp.tile(mem_qk_g[l, 1], 2)[None, :],
        e2=_gate_expand(), w_out=w_out[l].astype(BF16))


def _rope_tables(pos):
    half = MLA_ROPE // 2
    inv = ROPE_THETA ** (-jnp.arange(half, dtype=F32) / half)
    ang = pos.astype(F32)[:, None] * inv[None, :]
    c, s = jnp.cos(ang), jnp.sin(ang)
    n = pos.shape[0]
    cos = jnp.concatenate([jnp.ones((n, MLA_NOPE), F32), c, c, jnp.zeros((n, LANES - MLA_QK), F32)], axis=1)
    sin = jnp.concatenate([jnp.zeros((n, MLA_NOPE), F32), -s, s, jnp.zeros((n, LANES - MLA_QK), F32)], axis=1)
    return cos, sin


def _group_queries(qb):
    DB = qb.shape[0]
    qh = qb.reshape(DB, NSA_G, NSA_HPG, HD)
    out = jnp.zeros((DB, NSA_G, 8, 4, HD), qb.dtype)
    for g in range(NSA_G):
        out = out.at[:, g, :NSA_HPG, g].set(qh[:, g])
    return out.reshape(DB, NSA_G, 8, 4 * HD)


def kernel(x_prompt, x_sample, mem_prompt, cache_mla, cache_nsa_cmp, cache_nsa_slc, cache_nsa_win, cache_mem_kv,
           page_table, norm_g, w_in, mla_q_norm, mla_w_uq, mla_kv_norm, mla_w_ukv, mla_nope_g, mla_rope_g, nsa_qk_g,
           nsa_cmp_pe, nsa_cmp_w1, nsa_cmp_b1, nsa_cmp_w2, mem_norm_g, w_mem_kv, mem_qk_g, w_out, rel_bias):
    N, T, _ = x_prompt.shape
    DB = x_sample.shape[0]
    depth = norm_g.shape[0]
    n_pool = cache_mla.shape[1]
    npg = page_table.shape[1]
    past = npg * PAGE
    wb = cache_nsa_win.shape[2]
    assert x_sample.shape[1] == 1 and T % TQ == 0 and past % TQ == 0 and T // SEL_BLOCK <= LANES
    assert DB % 8 == 0 and wb == min(WINDOW, past) and wb >= REL_MAX_DIST and past // SEL_BLOCK >= 2
    npg_p = T // PAGE
    n_past_blk = past // SEL_BLOCK

    tf = TF if T % TF == 0 else TQ
    bias = _bias_tables(rel_bias, tf)
    bd = jnp.stack([bias[:, 0:tf], bias[:, tf:2 * tf]], axis=1)
    dcn = bias[:, 2 * tf:2 * tf + TQ, 0:LANES]
    dvec = bias[:, 2 * tf + TQ, 0:REL_MAX_DIST + 1]
    dvec = dvec.at[:, REL_MAX_DIST].set(0.0)
    hg = jnp.arange(NSA_G)[:, None] * NSA_HPG + jnp.minimum(jnp.arange(8), NSA_HPG - 1)[None, :]
    dg = dvec[hg]
    r64 = jnp.arange(SEL_BLOCK)
    z64 = jnp.zeros((NSA_G, 8, SEL_BLOCK), F32)
    b1 = dg[:, :, SEL_BLOCK - r64]
    b2 = dg[:, :, 2 * SEL_BLOCK - r64]
    in_page = lambda b, blk: jnp.concatenate([b, z64] if blk % (PAGE // SEL_BLOCK) == 0 else [z64, b], -1)
    bt = jnp.stack([in_page(b1, n_past_blk - 1), in_page(b2, n_past_blk - 2)], axis=1)
    d0 = jnp.broadcast_to(dg[:, :, 0:1], (NSA_G, 8, LANES))
    bw = dg[:, :, jnp.minimum(wb - jnp.arange(wb), REL_MAX_DIST)]

    def feat_major(a):
        a = jnp.moveaxis(a, -4, -1) if a.ndim >= 5 else jnp.swapaxes(a, -1, -2)
        return a.reshape(a.shape[:-4] + (-1, a.shape[-1])) if a.ndim >= 5 else a
    mla_t = feat_major(cache_mla).reshape(depth * n_pool, MLA_KVL + MLA_ROPE, PAGE)
    cmp_t = feat_major(cache_nsa_cmp).reshape(depth * n_pool, 2 * LANES, PAGE)
    slc_t = feat_major(cache_nsa_slc).reshape(depth * n_pool, 2 * LANES, PAGE)

    cos_p, sin_p = _rope_tables(jnp.arange(T))
    cos_s, sin_s = _rope_tables(jnp.full((DB,), past))
    pt_prompt = jnp.arange(N * npg_p, dtype=jnp.int32).reshape(N, npg_p)
    tm_p = TQ

    yp = x_prompt.reshape(N * T, D_MODEL)
    ys = x_sample.reshape(DB, D_MODEL)
    outs = [[] for _ in range(9)]
    for l in range(depth):
        lw = _layer_weights(l, norm_g, w_in, mla_q_norm, mla_w_uq, mla_kv_norm, mla_w_ukv, mla_nope_g, mla_rope_g,
                            nsa_qk_g, nsa_cmp_pe, nsa_cmp_w1, nsa_cmp_b1, nsa_cmp_w2, mem_norm_g, w_mem_kv,
                            mem_qk_g, w_out)
        p = _prep(yp, cos_p, sin_p, lw, tf)
        r3 = lambda a: a.reshape(N, T, a.shape[-1])
        o_a = _flash("mla", r3(p["qa"]), r3(p["ka"]), r3(p["va"]))
        kc4, vc4 = _compress(pt_prompt, p["rows_cmp"].reshape(N * npg_p, PAGE, 2 * LANES), lw, npg_p, False)
        o_c, sel = _cmp_topk(r3(p["qb"]), kc4, vc4, dcn, TQ, 0, T // SEL_BLOCK, True)
        o_s = _flash("slc", r3(p["qb"]), r3(p["ks4"]), r3(p["vs4"]), sel=sel, bd=bd)
        o_w = _flash("win", r3(p["qb"]), r3(p["kw4"]), r3(p["vw4"]), bd=bd)
        mkv = _mem_kv(mem_prompt, lw)
        o_m = _mem_attn(r3(p["qm"]), mkv, tm_p)
        f2 = lambda a: a.reshape(N * T, a.shape[-1])
        yp = _merge(yp, f2(o_a), f2(o_c), f2(o_s), f2(o_w), p["gb"], f2(o_m), p["gates"], lw, tm_p)
        outs[0].append(p["rows_a"].reshape(N, T, MLA_KVL + MLA_ROPE))
        outs[2].append(p["rows_cmp"].reshape(N, T, 2, NSA_G, HD))
        outs[4].append(p["rows_slc"].reshape(N, T, 2, NSA_G, HD))
        outs[6].append(p["rows_win"].reshape(N, T, 2, NSA_G, HD)[:, T - min(WINDOW, T):])
        outs[8].append(mkv.reshape(N, mkv.shape[1], 2, MEM_H, HD))
        ps = _prep(ys, cos_s, sin_s, lw, DB)
        pt_l = page_table + l * n_pool
        qa3 = ps["qa"].reshape(DB, MLA_H, LANES)
        q6 = (qa3[:, :, None, :MLA_NOPE] * jnp.eye(MLA_H, dtype=BF16)[None, :, :, None]).reshape(DB, MLA_H, MLA_H * HD)
        q6 = jnp.pad(q6, ((0, 0), (0, 8 - MLA_H), (0, 0)))
        qr = jnp.pad(qa3[:, :, MLA_NOPE:MLA_QK], ((0, 0), (0, 8 - MLA_H), (0, LANES - MLA_ROPE)))
        own8 = lambda a: jnp.pad(a[:, None, :], ((0, 0), (0, 7), (0, 0)))
        oa_s = _mla_decode(pt_l, mla_t, q6, qr, own8(ps["rows_a"]), lw, npg)[:, 0]
        kc4s, vc4s = _compress(pt_l, cmp_t, lw, npg, True)
        qb8 = jnp.pad(ps["qb"][:, None, :], ((0, 0), (0, 7), (0, 0)))
        oc_s, imp_s = _cmp_topk(qb8, kc4s, vc4s, dcn[:, 0:8], 8, past, n_past_blk + 1, False)
        idx = _topk(imp_s[:, 0], imp_s.shape[-1] // NSA_G)
        qg = jnp.pad(ps["qb"].astype(F32).reshape(DB, NSA_G, NSA_HPG, HD),
                     ((0, 0), (0, 0), (0, 8 - NSA_HPG), (0, 0))).reshape(DB, NSA_G * 8, HD)
        os_s = _slc_decode_t(idx[:, 0:NSA_G * TOP_N], pt_l, slc_t, qg, own8(ps["rows_slc"]), bt, d0, n_past_blk)
        ow_s = _win_decode_t(feat_major(cache_nsa_win[l]), qg, own8(ps["rows_win"]), bw, d0, wb - WINDOW + 1)
        qmh = ps["qm"].reshape(DB, MEM_H, HD)
        qm8 = (qmh[:, :, None, :] * jnp.eye(MEM_H, dtype=BF16)[None, :, :, None]).reshape(DB, MEM_H, MEM_H * HD)
        qm8 = jnp.pad(qm8, ((0, 0), (0, 8 - MEM_H), (0, 0)))
        om8 = _mem_decode_t(feat_major(cache_mem_kv[l]), qm8)
        ys = _merge(ys, oa_s, oc_s[:, 0], os_s[:, 0], ow_s[:, 0], ps["gb"], om8[:, 0], ps["gates"], lw, DB)
        outs[1].append(ps["rows_a"].reshape(DB, 1, MLA_KVL + MLA_ROPE))
        outs[3].append(ps["rows_cmp"].reshape(DB, 1, 2, NSA_G, HD))
        outs[5].append(ps["rows_slc"].reshape(DB, 1, 2, NSA_G, HD))
        new_win = jnp.concatenate([cache_nsa_win[l][:, 1:], ps["rows_win"].reshape(DB, 1, 2, NSA_G, HD)], axis=1)
        outs[7].append(new_win)
    return (yp.reshape(N, T, D_MODEL), ys.reshape(DB, 1, D_MODEL)) + tuple(jnp.stack(o) for o in outs)
```
